```python
import math
import jax, jax.numpy as jnp
from jax import lax
import numpy as np

D_MODEL = 2048
BATCH = 1
SEQ = 8192
DEPTH = 2

N_MEM = 256
GRID_W = 64
HEAD_DIM = 128
MIX_WIDTH = D_MODEL
IN_PROJ = 3 * MIX_WIDTH
N_HEADS_A = MIX_WIDTH // (2 * HEAD_DIM)
DK_A = HEAD_DIM // 2
N_HEADS_B = MIX_WIDTH // (2 * HEAD_DIM)
NA_KH = 8
NA_KW = 16
N_HEADS_C = MIX_WIDTH // HEAD_DIM
DILATED_PAIRS = ((128, 1), (512, 4), (2048, 16))
N_HEADS_MEM = 4
MEM_WIDTH = N_HEADS_MEM * HEAD_DIM
D_FF = ((8 * D_MODEL // 3 + 255) // 256) * 256
N_EVEN = (DEPTH + 1) // 2
Q_BLOCK = 128
EPS = 1e-6
NEG_INF = -1e30

kernel_name = 'hybrid_diff_na_dilated_macaron_encoder'


def rmsnorm(x, g):
    xf = x.astype(jnp.float32)
    y = xf * lax.rsqrt(jnp.mean(xf * xf, axis=-1, keepdims=True) + EPS)
    return (y * g.astype(jnp.float32)).astype(x.dtype)


def swiglu(x, w_gate, w_up, w_down):
    return (jax.nn.silu(x @ w_gate) * (x @ w_up)) @ w_down


def alibi_slopes(n):
    return jnp.asarray(np.exp2(-8.0 * np.arange(1, n + 1) / n).astype(np.float32))


def diff_attention(q, k, v, lam, subln, lambda_init):
    b, h, s, _, dk = q.shape
    nblk = s // Q_BLOCK
    slopes = alibi_slopes(h)[None, :, None, None, None]
    kf = k.astype(jnp.float32)
    vf = v.astype(jnp.float32)
    kpos = jnp.arange(s)
    qb = q.astype(jnp.float32).reshape(b, h, nblk, Q_BLOCK, 2, dk).transpose(2, 0, 1, 3, 4, 5)

    def one_block(args):
        qblk, start = args
        sc = jnp.einsum('bhqcd,bhkcd->bhcqk', qblk, kf) * (dk ** -0.5)
        qpos = start + jnp.arange(Q_BLOCK)
        dist = jnp.abs(qpos[:, None] - kpos[None, :]).astype(jnp.float32)
        p = jax.nn.softmax(sc - slopes * dist, axis=-1)
        w = p[:, :, 0] - lam * p[:, :, 1]
        return jnp.einsum('bhqk,bhkd->bhqd', w, vf)

    o = lax.map(one_block, (qb, jnp.arange(nblk) * Q_BLOCK))
    o = o.transpose(1, 2, 0, 3, 4).reshape(b, h, s, -1)
    return (rmsnorm(o, subln) * (1.0 - lambda_init)).astype(v.dtype)


def neighborhood_attention(q, k, v, rpb):
    b, h, s, dh = q.shape
    rows = s // GRID_W
    kh = min(NA_KH, rows)
    kw = NA_KW
    qg = q.reshape(b, h, rows, GRID_W, dh).astype(jnp.float32)
    kg = k.reshape(b, h, rows, GRID_W, dh).astype(jnp.float32)
    vg = v.reshape(b, h, rows, GRID_W, dh).astype(jnp.float32)
    r = jnp.arange(rows)
    c = jnp.arange(GRID_W)
    r0 = jnp.clip(r - kh // 2, 0, rows - kh)
    c0 = jnp.clip(c - kw // 2, 0, GRID_W - kw)
    row_idx = r0[:, None] + jnp.arange(kh)[None, :]
    kb = kg[:, :, row_idx]
    vb = vg[:, :, row_idx]
    sc = jnp.einsum('bhrqd,bhrjkd->bhrqjk', qg, kb) * (dh ** -0.5)
    dr = row_idx - r[:, None]
    dc = c[None, :] - c[:, None]
    bias = rpb.astype(jnp.float32)[:, (dr + NA_KH - 1)[:, None, :, None],
                                   jnp.clip(dc + NA_KW - 1, 0, 2 * NA_KW - 2)[None, :, None, :]]
    in_win = (c[None, :] >= c0[:, None]) & (c[None, :] < c0[:, None] + kw)
    sc = jnp.where(in_win[None, None, None, :, None, :], sc + bias[None], NEG_INF)
    p = jax.nn.softmax(sc.reshape(b, h, rows, GRID_W, kh * GRID_W), axis=-1)
    p = p.reshape(b, h, rows, GRID_W, kh, GRID_W)
    o = jnp.einsum('bhrqjk,bhrjkd->bhrqd', p, vb)
    return o.reshape(b, h, s, dh).astype(v.dtype)


def dilated_branch(q, k, v, window, dil, slopes):
    b, h, s, dh = q.shape
    L = s // dil
    half = window // (2 * dil)

    def split(t):
        return t.astype(jnp.float32).reshape(b, h, L, dil, dh).transpose(0, 1, 3, 2, 4)

    qs, ks, vs = split(q), split(k), split(v)
    nblk = -(-L // Q_BLOCK)
    lp = nblk * Q_BLOCK
    band = Q_BLOCK + 2 * half
    qs = jnp.pad(qs, ((0, 0), (0, 0), (0, 0), (0, lp - L), (0, 0)))
    kpad = ((0, 0), (0, 0), (0, 0), (half, lp - L + half), (0, 0))
    ks = jnp.pad(ks, kpad)
    vs = jnp.pad(vs, kpad)
    qb = qs.reshape(b, h, dil, nblk, Q_BLOCK, dh)
    band_idx = jnp.arange(nblk)[:, None] * Q_BLOCK + jnp.arange(band)[None, :]
    kb = ks[:, :, :, band_idx]
    vb = vs[:, :, :, band_idx]
    sc = jnp.einsum('bhgnqd,bhgnkd->bhgnqk', qb, kb) * (dh ** -0.5)
    qm = jnp.arange(nblk)[:, None] * Q_BLOCK + jnp.arange(Q_BLOCK)[None, :]
    km = band_idx - half
    delta = jnp.abs(km[:, None, :] - qm[:, :, None])
    valid = (delta <= half) & (km[:, None, :] >= 0) & (km[:, None, :] < L)
    dist = (delta * dil).astype(jnp.float32)
    sc = sc - slopes[None, :, None, None, None, None] * dist[None, None, None]
    sc = jnp.where(valid[None, None, None], sc, NEG_INF)
    lse = jax.nn.logsumexp(sc, axis=-1)
    p = jnp.exp(sc - lse[..., None])
    o = jnp.einsum('bhgnqk,bhgnkd->bhgnqd', p, vb)
    o = o.reshape(b, h, dil, lp, dh)[:, :, :, :L].transpose(0, 1, 3, 2, 4).reshape(b, h, s, dh)
    lse = lse.reshape(b, h, dil, lp)[:, :, :, :L].transpose(0, 1, 3, 2).reshape(b, h, s)
    return o, lse


def dilated_attention(q, k, v):
    slopes = alibi_slopes(q.shape[1])
    outs, lses = [], []
    for window, dil in DILATED_PAIRS:
        o, lse = dilated_branch(q, k, v, window, dil, slopes)
        outs.append(o)
        lses.append(lse)
    wts = jax.nn.softmax(jnp.stack(lses, axis=0), axis=0)
    o = jnp.einsum('gbhs,gbhsd->bhsd', wts, jnp.stack(outs, axis=0))
    return o.astype(v.dtype)


def mixer_even(h, w_in, w_out, lam, subln, rpb, lambda_init):
    b, s, _ = h.shape
    qa, ka, va, qn, kn, vn = jnp.split(h @ w_in, 6, axis=-1)

    def heads(t, n):
        return t.reshape(b, s, n, -1).transpose(0, 2, 1, 3)

    qa = qa.reshape(b, s, N_HEADS_A, 2, DK_A).transpose(0, 2, 1, 3, 4)
    ka = ka.reshape(b, s, N_HEADS_A, 2, DK_A).transpose(0, 2, 1, 3, 4)
    oa = diff_attention(qa, ka, heads(va, N_HEADS_A), lam, subln, lambda_init)
    ob = neighborhood_attention(heads(qn, N_HEADS_B), heads(kn, N_HEADS_B), heads(vn, N_HEADS_B), rpb)
    o = jnp.concatenate([oa, ob], axis=1).transpose(0, 2, 1, 3).reshape(b, s, MIX_WIDTH)
    return o @ w_out


def mixer_odd(h, w_in, w_out):
    b, s, _ = h.shape
    q, k, v = jnp.split(h @ w_in, 3, axis=-1)

    def heads(t):
        return t.reshape(b, s, N_HEADS_C, HEAD_DIM).transpose(0, 2, 1, 3)

    o = dilated_attention(heads(q), heads(k), heads(v))
    return o.transpose(0, 2, 1, 3).reshape(b, s, MIX_WIDTH) @ w_out


def memory_xattn(h, mem_n, wq, wkv, wo):
    b, s, _ = h.shape
    m = mem_n.shape[1]
    q = (h @ wq).reshape(b, s, N_HEADS_MEM, HEAD_DIM).astype(jnp.float32)
    kv = (mem_n @ wkv).reshape(b, m, 2, N_HEADS_MEM, HEAD_DIM).astype(jnp.float32)
    sc = jnp.einsum('bshd,bmhd->bhsm', q, kv[:, :, 0]) * (HEAD_DIM ** -0.5)
    p = jax.nn.softmax(sc, axis=-1)
    o = jnp.einsum('bhsm,bmhd->bshd', p, kv[:, :, 1]).astype(h.dtype).reshape(b, s, MEM_WIDTH)
    return o @ wo


def setup_inputs(seed: int = 0) -> dict:
    key = jax.random.key(seed)
    ks = jax.random.split(key, 32)
    f32 = jnp.float32

    def w(k, shape, fan_in):
        return jax.random.normal(k, shape, f32) * (fan_in ** -0.5)

    def gain(k, shape):
        return 1.0 + 0.02 * jax.random.normal(k, shape, f32)

    return {
        'x': jax.random.normal(ks[0], (BATCH, SEQ, D_MODEL), f32),
        'mem': jax.random.normal(ks[1], (BATCH, N_MEM, D_MODEL), f32),
        'ffn1_norm': gain(ks[2], (DEPTH, D_MODEL)),
        'ffn1_w_gate': w(ks[3], (DEPTH, D_MODEL, D_FF), D_MODEL),
        'ffn1_w_up': w(ks[4], (DEPTH, D_MODEL, D_FF), D_MODEL),
        'ffn1_w_down': w(ks[5], (DEPTH, D_FF, D_MODEL), D_FF),
        'mix_norm': gain(ks[6], (DEPTH, D_MODEL)),
        'mix_w_in': w(ks[7], (DEPTH, D_MODEL, IN_PROJ), D_MODEL),
        'mix_w_out': w(ks[8], (DEPTH, MIX_WIDTH, D_MODEL), MIX_WIDTH),
        'diff_lq1': 0.1 * jax.random.normal(ks[9], (N_EVEN, DK_A), f32),
        'diff_lk1': 0.1 * jax.random.normal(ks[10], (N_EVEN, DK_A), f32),
        'diff_lq2': 0.1 * jax.random.normal(ks[11], (N_EVEN, DK_A), f32),
        'diff_lk2': 0.1 * jax.random.normal(ks[12], (N_EVEN, DK_A), f32),
        'diff_subln': gain(ks[13], (N_EVEN, HEAD_DIM)),
        'na_rpb': 0.1 * jax.random.normal(ks[14], (N_EVEN, N_HEADS_B, 2 * NA_KH - 1, 2 * NA_KW - 1), f32),
        'mem_q_norm': gain(ks[15], (DEPTH, D_MODEL)),
        'mem_kv_norm': gain(ks[16], (DEPTH, D_MODEL)),
        'mem_wq': w(ks[17], (DEPTH, D_MODEL, MEM_WIDTH), D_MODEL),
        'mem_wkv': w(ks[18], (DEPTH, D_MODEL, 2 * MEM_WIDTH), D_MODEL),
        'mem_wo': w(ks[19], (DEPTH, MEM_WIDTH, D_MODEL), MEM_WIDTH),
        'ffn2_norm': gain(ks[20], (DEPTH, D_MODEL)),
        'ffn2_w_gate': w(ks[21], (DEPTH, D_MODEL, D_FF), D_MODEL),
        'ffn2_w_up': w(ks[22], (DEPTH, D_MODEL, D_FF), D_MODEL),
        'ffn2_w_down': w(ks[23], (DEPTH, D_FF, D_MODEL), D_FF),
        'final_norm': gain(ks[24], (D_MODEL,)),
    }


def reference(x, mem, ffn1_norm, ffn1_w_gate, ffn1_w_up, ffn1_w_down, mix_norm, mix_w_in, mix_w_out,
              diff_lq1, diff_lk1, diff_lq2, diff_lk2, diff_subln, na_rpb, mem_q_norm, mem_kv_norm,
              mem_wq, mem_wkv, mem_wo, ffn2_norm, ffn2_w_gate, ffn2_w_up, ffn2_w_down, final_norm):
    for i in range(DEPTH):
        x = x + 0.5 * swiglu(rmsnorm(x, ffn1_norm[i]), ffn1_w_gate[i], ffn1_w_up[i], ffn1_w_down[i])
        h = rmsnorm(x, mix_norm[i])
        if i % 2 == 0:
            e = i // 2
            lambda_init = 0.8 - 0.6 * math.exp(-0.3 * i)
            lam = (jnp.exp(jnp.sum(diff_lq1[e].astype(jnp.float32) * diff_lk1[e].astype(jnp.float32)))
                   - jnp.exp(jnp.sum(diff_lq2[e].astype(jnp.float32) * diff_lk2[e].astype(jnp.float32)))
                   + lambda_init)
            x = x + mixer_even(h, mix_w_in[i], mix_w_out[i], lam, diff_subln[e], na_rpb[e], lambda_init)
        else:
            x = x + mixer_odd(h, mix_w_in[i], mix_w_out[i])
        x = x + memory_xattn(rmsnorm(x, mem_q_norm[i]), rmsnorm(mem, mem_kv_norm[i]),
                             mem_wq[i], mem_wkv[i], mem_wo[i])
        x = x + 0.5 * swiglu(rmsnorm(x, ffn2_norm[i]), ffn2_w_gate[i], ffn2_w_up[i], ffn2_w_down[i])
    return rmsnorm(x, final_norm)
```

```python
import functools
import math

import numpy as np
import jax
import jax.numpy as jnp
from jax import lax
from jax.experimental import pallas as pl
from jax.experimental.pallas import tpu as pltpu

F32 = jnp.float32
BF16 = jnp.bfloat16

EPS = 1e-6
NEG_INF = -1e30
HEAD_DIM = 128
GRID_W = 64
NA_KH = 8
NA_KW = 16
N_HEADS_MEM = 4
DILATED_PAIRS = ((128, 1), (512, 4), (2048, 16))
Q_BLOCK = 128

VMEM_LIMIT = 52 * 1024 * 1024

_NT = (((1,), (1,)), ((), ()))


def _params(*sem):
    return pltpu.CompilerParams(dimension_semantics=sem, vmem_limit_bytes=VMEM_LIMIT)


def _rms(x, g):
    return x * lax.rsqrt(jnp.mean(x * x, axis=-1, keepdims=True) + EPS) * g


def _alibi_slopes(n):
    return np.exp2(-8.0 * np.arange(1, n + 1) / n).astype(np.float32)


def _ffn_body(x_ref, g_ref, wg_ref, wu_ref, wd_ref, fg_ref, o_ref, h_ref, *, final_norm):
    j = pl.program_id(1)

    @pl.when(j == 0)
    def _():
        x = x_ref[...]
        h_ref[...] = _rms(x, g_ref[...]).astype(BF16)
        o_ref[...] = x

    h = h_ref[...]
    a = jnp.dot(h, wg_ref[...], preferred_element_type=F32)
    b = jnp.dot(h, wu_ref[...], preferred_element_type=F32)
    act = (0.5 * a * jax.nn.sigmoid(a)) * b
    o_ref[...] += jnp.dot(act.astype(BF16), wd_ref[...], preferred_element_type=F32)

    if final_norm:
        @pl.when(j == pl.num_programs(1) - 1)
        def _():
            o_ref[...] = _rms(o_ref[...], fg_ref[...])


def _ffn(x, g, wg, wu, wd, final_g=None, *, tm=512, tf=512):
    s, d = x.shape
    f = wg.shape[1]
    final_norm = final_g is not None
    fg = final_g if final_norm else g
    return pl.pallas_call(
        functools.partial(_ffn_body, final_norm=final_norm),
        grid=(s // tm, f // tf),
        in_specs=[
            pl.BlockSpec((tm, d), lambda i, j: (i, 0)),
            pl.BlockSpec((1, d), lambda i, j: (0, 0)),
            pl.BlockSpec((d, tf), lambda i, j: (0, j)),
            pl.BlockSpec((d, tf), lambda i, j: (0, j)),
            pl.BlockSpec((tf, d), lambda i, j: (j, 0)),
            pl.BlockSpec((1, d), lambda i, j: (0, 0)),
        ],
        out_specs=pl.BlockSpec((tm, d), lambda i, j: (i, 0)),
        out_shape=jax.ShapeDtypeStruct((s, d), F32),
        scratch_shapes=[pltpu.VMEM((tm, d), BF16)],
        compiler_params=_params("parallel", "arbitrary"),
        name="ffn",
    )(x, g.reshape(1, d), wg, wu, wd, fg.reshape(1, d))


def _norm_proj_body(x_ref, g_ref, w_ref, o_ref, h_ref):
    @pl.when(pl.program_id(1) == 0)
    def _():
        h_ref[...] = _rms(x_ref[...], g_ref[...]).astype(BF16)

    o_ref[...] = jnp.dot(h_ref[...], w_ref[...], preferred_element_type=F32).astype(o_ref.dtype)


def _norm_proj(x, g, w, *, tm=512, tn=512):
    s, d = x.shape
    n = w.shape[1]
    tm = min(tm, s)
    return pl.pallas_call(
        _norm_proj_body,
        grid=(s // tm, n // tn),
        in_specs=[
            pl.BlockSpec((tm, d), lambda i, j: (i, 0)),
            pl.BlockSpec((1, d), lambda i, j: (0, 0)),
            pl.BlockSpec((d, tn), lambda i, j: (0, j)),
        ],
        out_specs=pl.BlockSpec((tm, tn), lambda i, j: (i, j)),
        out_shape=jax.ShapeDtypeStruct((s, n), BF16),
        scratch_shapes=[pltpu.VMEM((tm, d), BF16)],
        compiler_params=_params("parallel", "arbitrary"),
        name="norm_proj",
    )(x, g.reshape(1, d), w)


def _diff_body(slopes_ref, lam_ref, q_ref, k_ref, v_ref, sg_ref, o_ref, vt_ref, acc_ref,
               *, tq, tk, seq, out_scale):
    h = pl.program_id(0)
    i = pl.program_id(1)
    nk = seq // tk
    dk = HEAD_DIM // 2

    @pl.when(i == 0)
    def _():
        def tr(c, carry):
            r0 = pl.multiple_of(c * tk, tk)
            vt_ref[:, pl.ds(r0, tk)] = v_ref[pl.ds(r0, tk), :].astype(F32).T.astype(BF16)
            return carry
        lax.fori_loop(0, nk, tr, 0)

    slope = slopes_ref[h]
    q = q_ref[...]
    lane = lax.broadcasted_iota(jnp.int32, q.shape, 1)
    qs = q * (dk ** -0.5)
    zero = jnp.zeros_like(qs)
    q_half = (jnp.where(lane < dk, qs, zero), jnp.where(lane >= dk, qs, zero))

    rel = (lax.broadcasted_iota(jnp.int32, (tk, tq), 0)
           - lax.broadcasted_iota(jnp.int32, (tk, tq), 1)).astype(F32)

    acc_ref[...] = jnp.zeros_like(acc_ref)

    def step(kb, carry):
        r0 = pl.multiple_of(kb * tk, tk)
        k = k_ref[pl.ds(r0, tk), :]
        vt = vt_ref[:, pl.ds(r0, tk)]
        bias = slope * jnp.abs(rel + (kb * tk - i * tq).astype(F32))
        new = []
        for c in range(2):
            m_old, l_old = carry[2 * c], carry[2 * c + 1]
            s = lax.dot_general(k, q_half[c], _NT, preferred_element_type=F32) - bias
            m_new = jnp.maximum(m_old, jnp.max(s, axis=0, keepdims=True))
            alpha = jnp.exp(m_old - m_new)
            p = jnp.exp(s - m_new)
            l_new = alpha * l_old + jnp.sum(p, axis=0, keepdims=True)
            acc_ref[c] = alpha * acc_ref[c] + jnp.dot(vt, p.astype(BF16), preferred_element_type=F32)
            new += [m_new, l_new]
        return tuple(new)

    m_init = jnp.full((1, tq), NEG_INF, F32)
    l_init = jnp.zeros((1, tq), F32)
    m0, l0, m1, l1 = lax.fori_loop(0, nk, step, (m_init, l_init, m_init, l_init))

    lam = lam_ref[0]
    ot = acc_ref[0] / l0 - lam * (acc_ref[1] / l1)
    o = ot.T
    o = _rms(o, sg_ref[...]) * out_scale
    o_ref[...] = o.astype(o_ref.dtype)


def _diff_attention(qkv, lam, subln, lambda_init, *, n_heads, q_col, k_col, v_col, tq=512, tk=512):
    seq = qkv.shape[0]
    slopes = jnp.asarray(_alibi_slopes(n_heads))
    smem = pl.BlockSpec(memory_space=pltpu.SMEM)
    return pl.pallas_call(
        functools.partial(_diff_body, tq=tq, tk=tk, seq=seq, out_scale=1.0 - lambda_init),
        grid=(n_heads, seq // tq),
        in_specs=[
            smem,
            smem,
            pl.BlockSpec((tq, HEAD_DIM), lambda h, i: (i, q_col + h)),
            pl.BlockSpec((seq, HEAD_DIM), lambda h, i: (0, k_col + h)),
            pl.BlockSpec((seq, HEAD_DIM), lambda h, i: (0, v_col + h)),
            pl.BlockSpec((1, HEAD_DIM), lambda h, i: (0, 0)),
        ],
        out_specs=pl.BlockSpec((tq, HEAD_DIM), lambda h, i: (i, h)),
        out_shape=jax.ShapeDtypeStruct((seq, n_heads * HEAD_DIM), BF16),
        scratch_shapes=[pltpu.VMEM((HEAD_DIM, seq), BF16), pltpu.VMEM((2, HEAD_DIM, tq), F32)],
        compiler_params=_params("parallel", "arbitrary"),
        name="diff_attn",
    )(slopes, lam.reshape(1).astype(F32), qkv, qkv, qkv, subln.reshape(1, HEAD_DIM))


NA_ROWS_Q = 4
NA_ROWS_K = 12


def _na_bias_table(rpb, rows):
    kh, kw, w = NA_KH, NA_KW, GRID_W
    nblk = rows // NA_ROWS_Q
    cases = np.array([0, 1, nblk - 1])
    ws = np.clip(NA_ROWS_Q * cases - kh // 2, 0, rows - NA_ROWS_K)
    r = NA_ROWS_Q * cases[:, None] + np.arange(NA_ROWS_Q)[None, :]
    r0 = np.clip(r - kh // 2, 0, rows - kh)
    kr = ws[:, None] + np.arange(NA_ROWS_K)[None, :]
    row_ok = (kr[:, None, :] >= r0[:, :, None]) & (kr[:, None, :] < r0[:, :, None] + kh)
    dr = np.clip(kr[:, None, :] - r[:, :, None] + kh - 1, 0, 2 * kh - 2)
    c = np.arange(w)
    c0 = np.clip(c - kw // 2, 0, w - kw)
    col_ok = (c[None, :] >= c0[:, None]) & (c[None, :] < c0[:, None] + kw)
    dc = np.clip(c[None, :] - c[:, None] + kw - 1, 0, 2 * kw - 2)
    ok = row_ok[:, :, None, :, None] & col_ok[None, None, :, None, :]
    vals = rpb.astype(F32)[:, dr[:, :, None, :, None], dc[None, None, :, None, :]]
    bias = jnp.where(ok[None], vals, NEG_INF)
    return bias.reshape(rpb.shape[0], 3, NA_ROWS_Q * w, NA_ROWS_K * w)


def _na_body(q_ref, k_ref, v_ref, b_ref, o_ref, *, rows, scale):
    rb = pl.program_id(1)
    ws = jnp.clip(NA_ROWS_Q * rb - NA_KH // 2, 0, rows - NA_ROWS_K)
    start = pl.multiple_of(ws * GRID_W, GRID_W)
    nkeys = NA_ROWS_K * GRID_W
    k = k_ref[pl.ds(start, nkeys), :]
    v = v_ref[pl.ds(start, nkeys), :]
    s = lax.dot_general(q_ref[...], k, _NT, preferred_element_type=F32) * scale + b_ref[0, 0]
    m = jnp.max(s, axis=-1, keepdims=True)
    p = jnp.exp(s - m)
    l = jnp.sum(p, axis=-1, keepdims=True)
    o = jnp.dot(p.astype(BF16), v, preferred_element_type=F32) / l
    o_ref[...] = o.astype(o_ref.dtype)


def _neighborhood_attention(qkv, rpb, *, n_heads, q_col, k_col, v_col):
    seq = qkv.shape[0]
    rows = seq // GRID_W
    nblk = rows // NA_ROWS_Q
    tq = NA_ROWS_Q * GRID_W
    bias = _na_bias_table(rpb, rows)

    def case(rb):
        return jnp.where(rb == 0, 0, jnp.where(rb == nblk - 1, 2, 1))

    return pl.pallas_call(
        functools.partial(_na_body, rows=rows, scale=HEAD_DIM ** -0.5),
        grid=(n_heads, nblk),
        in_specs=[
            pl.BlockSpec((tq, HEAD_DIM), lambda h, rb: (rb, q_col + h)),
            pl.BlockSpec((seq, HEAD_DIM), lambda h, rb: (0, k_col + h)),
            pl.BlockSpec((seq, HEAD_DIM), lambda h, rb: (0, v_col + h)),
            pl.BlockSpec((1, 1, tq, NA_ROWS_K * GRID_W), lambda h, rb: (h, case(rb), 0, 0)),
        ],
        out_specs=pl.BlockSpec((tq, HEAD_DIM), lambda h, rb: (rb, h)),
        out_shape=jax.ShapeDtypeStruct((seq, n_heads * HEAD_DIM), BF16),
        compiler_params=_params("parallel", "arbitrary"),
        name="na_attn",
    )(qkv, qkv, qkv, bias)


def _dil_body(slopes_ref, q_ref, kp_ref, k_ref, kn_ref, vp_ref, v_ref, vn_ref, o_ref, lse_ref,
              kext_ref, vext_ref, *, tq, half, dil, length, scale):
    h = pl.program_id(0)
    c = pl.program_id(2)
    qb = Q_BLOCK
    band = qb + 2 * half

    kext_ref[0:half] = kp_ref[...]
    kext_ref[half:half + tq] = k_ref[...]
    kext_ref[half + tq:] = kn_ref[...]
    vext_ref[0:half] = vp_ref[...]
    vext_ref[half:half + tq] = v_ref[...]
    vext_ref[half + tq:] = vn_ref[...]

    slope = slopes_ref[h]
    bi = lax.broadcasted_iota(jnp.int32, (qb, band), 1)
    qi = lax.broadcasted_iota(jnp.int32, (qb, band), 0)
    delta = jnp.abs(bi - half - qi)
    in_band = delta <= half
    bias = -(slope * dil) * delta.astype(F32)

    for sub in range(tq // qb):
        km = c * tq + sub * qb - half + bi
        valid = in_band & (km >= 0) & (km < length)
        q = q_ref[sub * qb:(sub + 1) * qb, :]
        kb = kext_ref[sub * qb:sub * qb + band, :]
        vb = vext_ref[sub * qb:sub * qb + band, :]
        s = lax.dot_general(q, kb, _NT, preferred_element_type=F32) * scale + bias
        s = jnp.where(valid, s, NEG_INF)
        m = jnp.max(s, axis=-1, keepdims=True)
        p = jnp.exp(s - m)
        l = jnp.sum(p, axis=-1, keepdims=True)
        o = jnp.dot(p.astype(BF16), vb, preferred_element_type=F32) / l
        o_ref[sub * qb:(sub + 1) * qb, :] = o
        lse_ref[sub * qb:(sub + 1) * qb, :] = jnp.broadcast_to(m + jnp.log(l), (qb, HEAD_DIM))


def _dilated_branch(qkv, *, n_heads, window, dil):
    seq, width = qkv.shape
    d_model = n_heads * HEAD_DIM
    length = seq // dil
    half = window // (2 * dil)
    tq = min(length, 1024)
    nc = length // tq
    hb = tq // half
    last_hb = length // half - 1
    cols = width // HEAD_DIM
    view = qkv.reshape(length, dil * width)
    slopes = jnp.asarray(_alibi_slopes(n_heads))

    def col(off):
        return lambda h, r, c: (c, r * cols + off + h)

    def prev(off):
        return lambda h, r, c: (jnp.maximum(c * hb - 1, 0), r * cols + off + h)

    def nxt(off):
        return lambda h, r, c: (jnp.minimum((c + 1) * hb, last_hb), r * cols + off + h)

    ko, vo = n_heads, 2 * n_heads
    out_spec = pl.BlockSpec((tq, HEAD_DIM), lambda h, r, c: (c, r * n_heads + h))
    o, lse = pl.pallas_call(
        functools.partial(_dil_body, tq=tq, half=half, dil=dil, length=length, scale=HEAD_DIM ** -0.5),
        grid=(n_heads, dil, nc),
        in_specs=[
            pl.BlockSpec(memory_space=pltpu.SMEM),
            pl.BlockSpec((tq, HEAD_DIM), col(0)),
            pl.BlockSpec((half, HEAD_DIM), prev(ko)),
            pl.BlockSpec((tq, HEAD_DIM), col(ko)),
            pl.BlockSpec((half, HEAD_DIM), nxt(ko)),
            pl.BlockSpec((half, HEAD_DIM), prev(vo)),
            pl.BlockSpec((tq, HEAD_DIM), col(vo)),
            pl.BlockSpec((half, HEAD_DIM), nxt(vo)),
        ],
        out_specs=[out_spec, out_spec],
        out_shape=[jax.ShapeDtypeStruct((length, dil * d_model), F32)] * 2,
        scratch_shapes=[pltpu.VMEM((tq + 2 * half, HEAD_DIM), BF16)] * 2,
        compiler_params=_params("parallel", "parallel", "arbitrary"),
        name=f"dilated_d{dil}",
    )(slopes, view, view, view, view, view, view, view)
    return o.reshape(seq, d_model), lse.reshape(seq, d_model)


def _out_proj2_body(a_ref, b_ref, wa_ref, wb_ref, x_ref, o_ref):
    acc = jnp.dot(a_ref[...], wa_ref[...], preferred_element_type=F32)
    acc += jnp.dot(b_ref[...], wb_ref[...], preferred_element_type=F32)
    o_ref[...] = x_ref[...] + acc


def _out_proj2(a, b, w, x, *, tm=512, tn=512):
    s, d = x.shape
    ka = a.shape[1]
    kb = b.shape[1]
    return pl.pallas_call(
        _out_proj2_body,
        grid=(s // tm, d // tn),
        in_specs=[
            pl.BlockSpec((tm, ka), lambda i, j: (i, 0)),
            pl.BlockSpec((tm, kb), lambda i, j: (i, 0)),
            pl.BlockSpec((ka, tn), lambda i, j: (0, j)),
            pl.BlockSpec((kb, tn), lambda i, j: (ka // kb, j)),
            pl.BlockSpec((tm, tn), lambda i, j: (i, j)),
        ],
        out_specs=pl.BlockSpec((tm, tn), lambda i, j: (i, j)),
        out_shape=jax.ShapeDtypeStruct((s, d), F32),
        compiler_params=_params("parallel", "arbitrary"),
        name="out_proj2",
    )(a, b, w, w, x)


def _out_proj_mix_body(o1_ref, o2_ref, o3_ref, l1_ref, l2_ref, l3_ref, w_ref, x_ref, o_ref, a_ref):
    @pl.when(pl.program_id(1) == 0)
    def _():
        l1, l2, l3 = l1_ref[...], l2_ref[...], l3_ref[...]
        m = jnp.maximum(jnp.maximum(l1, l2), l3)
        e1, e2, e3 = jnp.exp(l1 - m), jnp.exp(l2 - m), jnp.exp(l3 - m)
        den = e1 + e2 + e3
        mix = (e1 / den) * o1_ref[...] + (e2 / den) * o2_ref[...] + (e3 / den) * o3_ref[...]
        a_ref[...] = mix.astype(BF16)

    o_ref[...] = x_ref[...] + jnp.dot(a_ref[...], w_ref[...], preferred_element_type=F32)


def _out_proj_mix(outs, lses, w, x, *, tm=256, tn=512):
    s, d = x.shape
    k = w.shape[0]
    row = pl.BlockSpec((tm, k), lambda i, j: (i, 0))
    return pl.pallas_call(
        _out_proj_mix_body,
        grid=(s // tm, d // tn),
        in_specs=[row] * 6 + [
            pl.BlockSpec((k, tn), lambda i, j: (0, j)),
            pl.BlockSpec((tm, tn), lambda i, j: (i, j)),
        ],
        out_specs=pl.BlockSpec((tm, tn), lambda i, j: (i, j)),
        out_shape=jax.ShapeDtypeStruct((s, d), F32),
        scratch_shapes=[pltpu.VMEM((tm, k), BF16)],
        compiler_params=_params("parallel", "arbitrary"),
        name="out_proj_mix",
    )(*outs, *lses, w, x)


def _mem_body(x_ref, g_ref, wq_ref, kv_ref, wo_ref, o_ref, *, scale):
    x = x_ref[...]
    h = _rms(x, g_ref[...]).astype(BF16)
    q = jnp.dot(h, wq_ref[...], preferred_element_type=F32).astype(BF16)
    width = N_HEADS_MEM * HEAD_DIM
    outs = []
    for hd in range(N_HEADS_MEM):
        lo, hi = hd * HEAD_DIM, (hd + 1) * HEAD_DIM
        s = lax.dot_general(q[:, lo:hi], kv_ref[:, lo:hi], _NT, preferred_element_type=F32) * scale
        m = jnp.max(s, axis=-1, keepdims=True)
        p = jnp.exp(s - m)
        l = jnp.sum(p, axis=-1, keepdims=True)
        o = jnp.dot(p.astype(BF16), kv_ref[:, width + lo:width + hi], preferred_element_type=F32) / l
        outs.append(o.astype(BF16))
    o = jnp.concatenate(outs, axis=-1)
    o_ref[...] = x + jnp.dot(o, wo_ref[...], preferred_element_type=F32)


def _mem_xattn(x, g, wq, kv, wo, *, tm=512):
    s, d = x.shape
    n_mem, kvw = kv.shape
    width = wq.shape[1]
    return pl.pallas_call(
        functools.partial(_mem_body, scale=HEAD_DIM ** -0.5),
        grid=(s // tm,),
        in_specs=[
            pl.BlockSpec((tm, d), lambda i: (i, 0)),
            pl.BlockSpec((1, d), lambda i: (0, 0)),
            pl.BlockSpec((d, width), lambda i: (0, 0)),
            pl.BlockSpec((n_mem, kvw), lambda i: (0, 0)),
            pl.BlockSpec((width, d), lambda i: (0, 0)),
        ],
        out_specs=pl.BlockSpec((tm, d), lambda i: (i, 0)),
        out_shape=jax.ShapeDtypeStruct((s, d), F32),
        compiler_params=_params("parallel"),
        name="mem_xattn",
    )(x, g.reshape(1, d), wq, kv, wo)


def kernel(x, mem, ffn1_norm, ffn1_w_gate, ffn1_w_up, ffn1_w_down, mix_norm, mix_w_in, mix_w_out, diff_lq1, diff_lk1, diff_lq2, diff_lk2, diff_subln, na_rpb, mem_q_norm, mem_kv_norm, mem_wq, mem_wkv, mem_wo, ffn2_norm, ffn2_w_gate, ffn2_w_up, ffn2_w_down, final_norm):
    batch, seq, d_model = x.shape
    assert batch == 1
    depth = ffn1_norm.shape[0]
    n_heads = d_model // HEAD_DIM
    xs = x.reshape(seq, d_model)
    mem2 = mem.reshape(mem.shape[1], d_model)
    bf = lambda t: t.astype(BF16)

    for i in range(depth):
        xs = _ffn(xs, ffn1_norm[i], bf(ffn1_w_gate[i]), bf(ffn1_w_up[i]), bf(ffn1_w_down[i]))
        qkv = _norm_proj(xs, mix_norm[i], bf(mix_w_in[i]))
        w_out = bf(mix_w_out[i])
        if i % 2 == 0:
            e = i // 2
            hh = n_heads // 2
            lambda_init = 0.8 - 0.6 * math.exp(-0.3 * i)
            lam = (jnp.exp(jnp.sum(diff_lq1[e].astype(F32) * diff_lk1[e].astype(F32)))
                   - jnp.exp(jnp.sum(diff_lq2[e].astype(F32) * diff_lk2[e].astype(F32)))
                   + lambda_init)
            oa = _diff_attention(qkv, lam, diff_subln[e], lambda_init, n_heads=hh,
                                 q_col=0, k_col=hh, v_col=2 * hh)
            ob = _neighborhood_attention(qkv, na_rpb[e], n_heads=hh,
                                         q_col=3 * hh, k_col=4 * hh, v_col=5 * hh)
            xs = _out_proj2(oa, ob, w_out, xs)
        else:
            outs, lses = [], []
            for window, dil in DILATED_PAIRS:
                o, lse = _dilated_branch(qkv, n_heads=n_heads, window=window, dil=dil)
                outs.append(o)
                lses.append(lse)
            xs = _out_proj_mix(outs, lses, w_out, xs)
        kv = _norm_proj(mem2, mem_kv_norm[i], bf(mem_wkv[i]))
        xs = _mem_xattn(xs, mem_q_norm[i], bf(mem_wq[i]), kv, bf(mem_wo[i]))
        last = i == depth - 1
        xs = _ffn(xs, ffn2_norm[i], bf(ffn2_w_gate[i]), bf(ffn2_w_up[i]), bf(ffn2_w_down[i]),
                  final_g=final_norm if last else None)
    return xs.reshape(batch, seq, d_model)
```

```python
import functools
import math

import numpy as np
import jax
import jax.numpy as jnp
from jax import lax
from jax.experimental import pallas as pl
from jax.experimental.pallas import tpu as pltpu

F32 = jnp.float32
BF16 = jnp.bfloat16

EPS = 1e-6
NEG_INF = -1e30
LOG2E = math.log2(math.e)
HEAD_DIM = 128
GRID_W = 64
NA_KH = 8
NA_KW = 16
N_HEADS_MEM = 4
DILATED_PAIRS = ((128, 1), (512, 4), (2048, 16))

VMEM_LIMIT = 52 * 1024 * 1024

_NT = (((1,), (1,)), ((), ()))


def _params(*sem):
    return pltpu.CompilerParams(dimension_semantics=sem, vmem_limit_bytes=VMEM_LIMIT)


def _rms(x, g):
    return x * lax.rsqrt(jnp.mean(x * x, axis=-1, keepdims=True) + EPS) * g


def _alibi_slopes(n):
    return np.exp2(-8.0 * np.arange(1, n + 1) / n).astype(np.float32)


def _layer_spec(layer, rows, cols, index_map):
    return pl.BlockSpec((None, rows, cols), lambda *g: (layer,) + tuple(index_map(*g)))


def _ffn_body(x_ref, g_ref, wg_ref, wu_ref, wd_ref, fg_ref, o_ref, h_ref, *, final_norm):
    j = pl.program_id(1)

    @pl.when(j == 0)
    def _():
        x = x_ref[...]
        h_ref[...] = _rms(x, g_ref[...]).astype(BF16)
        o_ref[...] = x

    h = h_ref[...]
    a = jnp.dot(h, wg_ref[...], preferred_element_type=F32)
    b = jnp.dot(h, wu_ref[...], preferred_element_type=F32)
    act = (0.5 * a * jax.nn.sigmoid(a)) * b
    o_ref[...] += jnp.dot(act.astype(BF16), wd_ref[...], preferred_element_type=F32)

    if final_norm:
        @pl.when(j == pl.num_programs(1) - 1)
        def _():
            o_ref[...] = _rms(o_ref[...], fg_ref[...])


def _ffn(x, g, wg, wu, wd, layer, final_g=None, *, tm=512, tf=512):
    s, d = x.shape
    f = wg.shape[2]
    final_norm = final_g is not None
    fg = final_g if final_norm else g
    return pl.pallas_call(
        functools.partial(_ffn_body, final_norm=final_norm),
        grid=(s // tm, f // tf),
        in_specs=[
            pl.BlockSpec((tm, d), lambda i, j: (i, 0)),
            pl.BlockSpec((1, d), lambda i, j: (0, 0)),
            _layer_spec(layer, d, tf, lambda i, j: (0, j)),
            _layer_spec(layer, d, tf, lambda i, j: (0, j)),
            _layer_spec(layer, tf, d, lambda i, j: (j, 0)),
            pl.BlockSpec((1, d), lambda i, j: (0, 0)),
        ],
        out_specs=pl.BlockSpec((tm, d), lambda i, j: (i, 0)),
        out_shape=jax.ShapeDtypeStruct((s, d), F32),
        scratch_shapes=[pltpu.VMEM((tm, d), BF16)],
        compiler_params=_params("parallel", "arbitrary"),
        name="ffn",
    )(x, g.reshape(1, d), wg, wu, wd, fg.reshape(1, d))


def _norm_proj_body(x_ref, g_ref, w_ref, cs_ref, o_ref, h_ref):
    @pl.when(pl.program_id(1) == 0)
    def _():
        h_ref[...] = _rms(x_ref[...], g_ref[...]).astype(BF16)

    acc = jnp.dot(h_ref[...], w_ref[...], preferred_element_type=F32)
    o_ref[...] = (acc * cs_ref[...]).astype(o_ref.dtype)


def _norm_proj(x, g, w, layer, col_scale, *, tm=512, tn=512):
    s, d = x.shape
    n = w.shape[2]
    tm = min(tm, s)
    return pl.pallas_call(
        _norm_proj_body,
        grid=(s // tm, n // tn),
        in_specs=[
            pl.BlockSpec((tm, d), lambda i, j: (i, 0)),
            pl.BlockSpec((1, d), lambda i, j: (0, 0)),
            _layer_spec(layer, d, tn, lambda i, j: (0, j)),
            pl.BlockSpec((1, tn), lambda i, j: (0, j)),
        ],
        out_specs=pl.BlockSpec((tm, tn), lambda i, j: (i, j)),
        out_shape=jax.ShapeDtypeStruct((s, n), BF16),
        scratch_shapes=[pltpu.VMEM((tm, d), BF16)],
        compiler_params=_params("parallel", "arbitrary"),
        name="norm_proj",
    )(x, g.reshape(1, d), w, jnp.asarray(col_scale, F32).reshape(1, n))


def _col_scale(n, scaled):
    cs = np.ones((n,), np.float32)
    for lo, hi, val in scaled:
        cs[lo:hi] = val
    return cs


def _transpose_rows_to_lanes(src_ref, dst_ref, seq, chunk):
    def body(c, carry):
        r0 = pl.multiple_of(c * chunk, chunk)
        dst_ref[:, pl.ds(r0, chunk)] = src_ref[pl.ds(r0, chunk), :].astype(F32).T.astype(BF16)
        return carry
    lax.fori_loop(0, seq // chunk, body, 0)


def _diff_body(slopes_ref, lam_ref, q_ref, k_ref, v_ref, sg_ref, o_ref, vt_ref, r_ref, acc_ref,
               *, t, seq, out_scale):
    h = pl.program_id(0)
    i = pl.program_id(1)
    nk = seq // t
    dk = HEAD_DIM // 2
    slope = slopes_ref[h]

    @pl.when(i == 0)
    def _():
        _transpose_rows_to_lanes(v_ref, vt_ref, seq, t)
        rel = (lax.broadcasted_iota(jnp.int32, (t, t), 0)
               - lax.broadcasted_iota(jnp.int32, (t, t), 1)).astype(F32)
        r_ref[...] = slope * rel

    q = q_ref[...]
    lane = lax.broadcasted_iota(jnp.int32, q.shape, 1)
    zero = jnp.zeros_like(q)
    q_half = (jnp.where(lane < dk, q, zero), jnp.where(lane >= dk, q, zero))
    acc_ref[...] = jnp.zeros_like(acc_ref)

    def make_step(side):
        def step(kb, carry):
            r0 = pl.multiple_of(kb * t, t)
            k = k_ref[pl.ds(r0, t), :]
            vt = vt_ref[:, pl.ds(r0, t)]
            off = slope * (jnp.abs(kb - i) * t).astype(F32)
            r = r_ref[...]
            new = []
            for c in range(2):
                m_old, l_old = carry[2 * c], carry[2 * c + 1]
                s = lax.dot_general(k, q_half[c], _NT, preferred_element_type=F32)
                if side < 0:
                    u = s + r
                elif side > 0:
                    u = s - r
                else:
                    u = s - jnp.abs(r)
                m_new = jnp.maximum(m_old, jnp.max(u, axis=0, keepdims=True) - off)
                alpha = jnp.exp2(m_old - m_new)
                p = jnp.exp2(u - (m_new + off))
                l_new = alpha * l_old + jnp.sum(p, axis=0, keepdims=True)
                acc_ref[c] = alpha * acc_ref[c] + jnp.dot(vt, p.astype(BF16), preferred_element_type=F32)
                new += [m_new, l_new]
            return tuple(new)
        return step

    m_init = jnp.full((1, t), NEG_INF, F32)
    l_init = jnp.zeros((1, t), F32)
    carry = (m_init, l_init, m_init, l_init)
    carry = lax.fori_loop(0, i, make_step(-1), carry)
    carry = make_step(0)(i, carry)
    m0, l0, m1, l1 = lax.fori_loop(i + 1, nk, make_step(1), carry)

    lam = lam_ref[0]
    ot = acc_ref[0] / l0 - lam * (acc_ref[1] / l1)
    o = _rms(ot.T, sg_ref[...]) * out_scale
    o_ref[...] = o.astype(o_ref.dtype)


def _diff_attention(qkv, lam, subln, lambda_init, *, n_heads, q_col, k_col, v_col, t=512):
    seq = qkv.shape[0]
    slopes = jnp.asarray(_alibi_slopes(n_heads) * np.float32(LOG2E))
    smem = pl.BlockSpec(memory_space=pltpu.SMEM)
    return pl.pallas_call(
        functools.partial(_diff_body, t=t, seq=seq, out_scale=1.0 - lambda_init),
        grid=(n_heads, seq // t),
        in_specs=[
            smem,
            smem,
            pl.BlockSpec((t, HEAD_DIM), lambda h, i: (i, q_col + h)),
            pl.BlockSpec((seq, HEAD_DIM), lambda h, i: (0, k_col + h)),
            pl.BlockSpec((seq, HEAD_DIM), lambda h, i: (0, v_col + h)),
            pl.BlockSpec((1, HEAD_DIM), lambda h, i: (0, 0)),
        ],
        out_specs=pl.BlockSpec((t, HEAD_DIM), lambda h, i: (i, h)),
        out_shape=jax.ShapeDtypeStruct((seq, n_heads * HEAD_DIM), BF16),
        scratch_shapes=[pltpu.VMEM((HEAD_DIM, seq), BF16), pltpu.VMEM((t, t), F32),
                        pltpu.VMEM((2, HEAD_DIM, t), F32)],
        compiler_params=_params("parallel", "arbitrary"),
        name="diff_attn",
    )(slopes, lam.reshape(1).astype(F32), qkv, qkv, qkv, subln.reshape(1, HEAD_DIM))


NA_ROWS_Q = 4
NA_ROWS_K = 12


def _na_bias_table(rpb, rows):
    kh, kw, w = NA_KH, NA_KW, GRID_W
    n_heads = rpb.shape[0]
    nblk = rows // NA_ROWS_Q
    rpb = rpb.astype(F32) * LOG2E
    ext_idx = np.clip(np.arange(2 * w - 1) - (w - 1) + kw - 1, 0, 2 * kw - 2)
    lo = int(np.argmax(ext_idx > 0))
    hi = int(np.argmax(ext_idx == 2 * kw - 2))
    ext = jnp.concatenate([jnp.repeat(rpb[..., :1], lo - 1, axis=-1), rpb,
                           jnp.repeat(rpb[..., -1:], 2 * w - 1 - hi - 1, axis=-1)], axis=-1)
    tab = jnp.stack([ext[..., w - 1 - qc:2 * w - 1 - qc] for qc in range(w)], axis=-2)
    c = np.arange(w)
    c0 = np.clip(c - kw // 2, 0, w - kw)
    col_ok = (c[None, :] >= c0[:, None]) & (c[None, :] < c0[:, None] + kw)
    tab = jnp.where(col_ok, tab, NEG_INF)
    masked = jnp.full((n_heads, w, w), NEG_INF, F32)
    cases = []
    for blk in (0, 1, nblk - 1):
        ws = int(np.clip(NA_ROWS_Q * blk - kh // 2, 0, rows - NA_ROWS_K))
        q_rows = []
        for a in range(NA_ROWS_Q):
            r = NA_ROWS_Q * blk + a
            r0 = int(np.clip(r - kh // 2, 0, rows - kh))
            tiles = []
            for wi in range(NA_ROWS_K):
                kr = ws + wi
                tiles.append(tab[:, kr - r + kh - 1] if r0 <= kr < r0 + kh else masked)
            q_rows.append(jnp.concatenate(tiles, axis=-1))
        cases.append(jnp.concatenate(q_rows, axis=-2))
    return jnp.stack(cases, axis=1)


def _na_body(q_ref, k_ref, v_ref, b_ref, o_ref, *, rows):
    rb = pl.program_id(1)
    ws = jnp.clip(NA_ROWS_Q * rb - NA_KH // 2, 0, rows - NA_ROWS_K)
    start = pl.multiple_of(ws * GRID_W, GRID_W)
    nkeys = NA_ROWS_K * GRID_W
    k = k_ref[pl.ds(start, nkeys), :]
    v = v_ref[pl.ds(start, nkeys), :]
    s = lax.dot_general(q_ref[...], k, _NT, preferred_element_type=F32) + b_ref[0, 0]
    m = jnp.max(s, axis=-1, keepdims=True)
    p = jnp.exp2(s - m)
    l = jnp.sum(p, axis=-1, keepdims=True)
    o = jnp.dot(p.astype(BF16), v, preferred_element_type=F32) / l
    o_ref[...] = o.astype(o_ref.dtype)


def _neighborhood_attention(qkv, rpb, *, n_heads, q_col, k_col, v_col):
    seq = qkv.shape[0]
    rows = seq // GRID_W
    nblk = rows // NA_ROWS_Q
    tq = NA_ROWS_Q * GRID_W
    bias = _na_bias_table(rpb, rows)

    def case(rb):
        return jnp.where(rb == 0, 0, jnp.where(rb == nblk - 1, 2, 1))

    return pl.pallas_call(
        functools.partial(_na_body, rows=rows),
        grid=(n_heads, nblk),
        in_specs=[
            pl.BlockSpec((tq, HEAD_DIM), lambda h, rb: (rb, q_col + h)),
            pl.BlockSpec((seq, HEAD_DIM), lambda h, rb: (0, k_col + h)),
            pl.BlockSpec((seq, HEAD_DIM), lambda h, rb: (0, v_col + h)),
            pl.BlockSpec((1, 1, tq, NA_ROWS_K * GRID_W), lambda h, rb: (h, case(rb), 0, 0)),
        ],
        out_specs=pl.BlockSpec((tq, HEAD_DIM), lambda h, rb: (rb, h)),
        out_shape=jax.ShapeDtypeStruct((seq, n_heads * HEAD_DIM), BF16),
        compiler_params=_params("parallel", "arbitrary"),
        name="na_attn",
    )(qkv, qkv, qkv, bias)


def _dil_body(slopes_ref, q_ref, k_ref, v_ref, o_ref, vt_ref, b_ref, acc_ref, *, t, seq, nb):
    h = pl.program_id(0)
    i = pl.program_id(1)
    nk = seq // t
    slope = slopes_ref[h]

    @pl.when(i == 0)
    def _():
        _transpose_rows_to_lanes(v_ref, vt_ref, seq, t)
        rel = (lax.broadcasted_iota(jnp.int32, (t, t), 0)
               - lax.broadcasted_iota(jnp.int32, (t, t), 1))
        for dlt in range(-nb, nb + 1):
            ad = jnp.abs(rel + dlt * t)
            cnt = jnp.zeros((t, t), jnp.int32)
            for window, dil in DILATED_PAIRS:
                ok = (ad <= window // 2) & ((ad & (dil - 1)) == 0)
                cnt = cnt + jnp.where(ok, 1, 0)
            logc = jnp.where(cnt == 3, math.log2(3.0), jnp.where(cnt == 2, 1.0, 0.0))
            b_ref[dlt + nb] = jnp.where(cnt > 0, logc - slope * ad.astype(F32), NEG_INF)

    q = q_ref[...]
    acc_ref[...] = jnp.zeros_like(acc_ref)

    def step(kb, carry):
        m_old, l_old = carry
        r0 = pl.multiple_of(kb * t, t)
        k = k_ref[pl.ds(r0, t), :]
        vt = vt_ref[:, pl.ds(r0, t)]
        u = lax.dot_general(k, q, _NT, preferred_element_type=F32) + b_ref[kb - i + nb]
        m_new = jnp.maximum(m_old, jnp.max(u, axis=0, keepdims=True))
        alpha = jnp.exp2(m_old - m_new)
        p = jnp.exp2(u - m_new)
        l_new = alpha * l_old + jnp.sum(p, axis=0, keepdims=True)
        acc_ref[...] = alpha * acc_ref[...] + jnp.dot(vt, p.astype(BF16), preferred_element_type=F32)
        return m_new, l_new

    init = (jnp.full((1, t), NEG_INF, F32), jnp.zeros((1, t), F32))
    _, l = lax.fori_loop(jnp.maximum(i - nb, 0), jnp.minimum(i + nb + 1, nk), step, init)
    o_ref[...] = (acc_ref[...] / l).T.astype(o_ref.dtype)


def _dilated_attention(qkv, *, n_heads, t=512):
    seq = qkv.shape[0]
    for window, dil in DILATED_PAIRS:
        assert dil & (dil - 1) == 0 and window % (2 * dil) == 0
    reach = max(window // 2 for window, _ in DILATED_PAIRS)
    nb = (reach - 1) // t + 1
    slopes = jnp.asarray(_alibi_slopes(n_heads) * np.float32(LOG2E))
    return pl.pallas_call(
        functools.partial(_dil_body, t=t, seq=seq, nb=nb),
        grid=(n_heads, seq // t),
        in_specs=[
            pl.BlockSpec(memory_space=pltpu.SMEM),
            pl.BlockSpec((t, HEAD_DIM), lambda h, i: (i, h)),
            pl.BlockSpec((seq, HEAD_DIM), lambda h, i: (0, n_heads + h)),
            pl.BlockSpec((seq, HEAD_DIM), lambda h, i: (0, 2 * n_heads + h)),
        ],
        out_specs=pl.BlockSpec((t, HEAD_DIM), lambda h, i: (i, h)),
        out_shape=jax.ShapeDtypeStruct((seq, n_heads * HEAD_DIM), BF16),
        scratch_shapes=[pltpu.VMEM((HEAD_DIM, seq), BF16), pltpu.VMEM((2 * nb + 1, t, t), F32),
                        pltpu.VMEM((HEAD_DIM, t), F32)],
        compiler_params=_params("parallel", "arbitrary"),
        name="dilated_attn",
    )(slopes, qkv, qkv, qkv)


def _out_proj_body(a_ref, b_ref, wa_ref, wb_ref, x_ref, o_ref):
    acc = jnp.dot(a_ref[...], wa_ref[...], preferred_element_type=F32)
    acc += jnp.dot(b_ref[...], wb_ref[...], preferred_element_type=F32)
    o_ref[...] = x_ref[...] + acc


def _out_proj(a, a_col, b, b_col, w, layer, x, *, tm=512, tn=512):
    s, d = x.shape
    kh = w.shape[1] // 2
    return pl.pallas_call(
        _out_proj_body,
        grid=(s // tm, d // tn),
        in_specs=[
            pl.BlockSpec((tm, kh), lambda i, j: (i, a_col)),
            pl.BlockSpec((tm, kh), lambda i, j: (i, b_col)),
            _layer_spec(layer, kh, tn, lambda i, j: (0, j)),
            _layer_spec(layer, kh, tn, lambda i, j: (1, j)),
            pl.BlockSpec((tm, tn), lambda i, j: (i, j)),
        ],
        out_specs=pl.BlockSpec((tm, tn), lambda i, j: (i, j)),
        out_shape=jax.ShapeDtypeStruct((s, d), F32),
        compiler_params=_params("parallel", "arbitrary"),
        name="out_proj",
    )(a, b, w, w, x)


def _mem_body(x_ref, g_ref, wq_ref, kv_ref, wo_ref, o_ref, *, scale):
    x = x_ref[...]
    h = _rms(x, g_ref[...]).astype(BF16)
    q = jnp.dot(h, wq_ref[...], preferred_element_type=F32).astype(BF16)
    width = N_HEADS_MEM * HEAD_DIM
    outs = []
    for hd in range(N_HEADS_MEM):
        lo, hi = hd * HEAD_DIM, (hd + 1) * HEAD_DIM
        s = lax.dot_general(q[:, lo:hi], kv_ref[:, lo:hi], _NT, preferred_element_type=F32) * scale
        m = jnp.max(s, axis=-1, keepdims=True)
        p = jnp.exp(s - m)
        l = jnp.sum(p, axis=-1, keepdims=True)
        o = jnp.dot(p.astype(BF16), kv_ref[:, width + lo:width + hi], preferred_element_type=F32) / l
        outs.append(o.astype(BF16))
    o = jnp.concatenate(outs, axis=-1)
    o_ref[...] = x + jnp.dot(o, wo_ref[...], preferred_element_type=F32)


def _mem_xattn(x, g, wq, kv, wo, layer, *, tm=512):
    s, d = x.shape
    n_mem, kvw = kv.shape
    width = wq.shape[2]
    return pl.pallas_call(
        functools.partial(_mem_body, scale=HEAD_DIM ** -0.5),
        grid=(s // tm,),
        in_specs=[
            pl.BlockSpec((tm, d), lambda i: (i, 0)),
            pl.BlockSpec((1, d), lambda i: (0, 0)),
            _layer_spec(layer, d, width, lambda i: (0, 0)),
            pl.BlockSpec((n_mem, kvw), lambda i: (0, 0)),
            _layer_spec(layer, width, d, lambda i: (0, 0)),
        ],
        out_specs=pl.BlockSpec((tm, d), lambda i: (i, 0)),
        out_shape=jax.ShapeDtypeStruct((s, d), F32),
        compiler_params=_params("parallel"),
        name="mem_xattn",
    )(x, g.reshape(1, d), wq, kv, wo)


def kernel(x, mem, ffn1_norm, ffn1_w_gate, ffn1_w_up, ffn1_w_down, mix_norm, mix_w_in, mix_w_out, diff_lq1, diff_lk1, diff_lq2, diff_lk2, diff_subln, na_rpb, mem_q_norm, mem_kv_norm, mem_wq, mem_wkv, mem_wo, ffn2_norm, ffn2_w_gate, ffn2_w_up, ffn2_w_down, final_norm):
    batch, seq, d_model = x.shape
    assert batch == 1
    depth = ffn1_norm.shape[0]
    n_heads = d_model // HEAD_DIM
    in_proj = mix_w_in.shape[2]
    xs = x.reshape(seq, d_model)
    mem2 = mem.reshape(mem.shape[1], d_model)

    bf = lambda t: t.astype(BF16)
    w1g, w1u, w1d = bf(ffn1_w_gate), bf(ffn1_w_up), bf(ffn1_w_down)
    w2g, w2u, w2d = bf(ffn2_w_gate), bf(ffn2_w_up), bf(ffn2_w_down)
    w_in, w_out = bf(mix_w_in), bf(mix_w_out)
    wq, wkv, wo = bf(mem_wq), bf(mem_wkv), bf(mem_wo)

    hh = n_heads // 2
    half_w = hh * HEAD_DIM
    cs_even = _col_scale(in_proj, [(0, half_w, (HEAD_DIM // 2) ** -0.5 * LOG2E),
                                   (3 * half_w, 4 * half_w, HEAD_DIM ** -0.5 * LOG2E)])
    cs_odd = _col_scale(in_proj, [(0, d_model, HEAD_DIM ** -0.5 * LOG2E)])
    cs_kv = _col_scale(mem_wkv.shape[2], [])

    for i in range(depth):
        xs = _ffn(xs, ffn1_norm[i], w1g, w1u, w1d, i)
        if i % 2 == 0:
            e = i // 2
            qkv = _norm_proj(xs, mix_norm[i], w_in, i, cs_even)
            lambda_init = 0.8 - 0.6 * math.exp(-0.3 * i)
            lam = (jnp.exp(jnp.sum(diff_lq1[e].astype(F32) * diff_lk1[e].astype(F32)))
                   - jnp.exp(jnp.sum(diff_lq2[e].astype(F32) * diff_lk2[e].astype(F32)))
                   + lambda_init)
            oa = _diff_attention(qkv, lam, diff_subln[e], lambda_init, n_heads=hh,
                                 q_col=0, k_col=hh, v_col=2 * hh)
            ob = _neighborhood_attention(qkv, na_rpb[e], n_heads=hh,
                                         q_col=3 * hh, k_col=4 * hh, v_col=5 * hh)
            xs = _out_proj(oa, 0, ob, 0, w_out, i, xs)
        else:
            qkv = _norm_proj(xs, mix_norm[i], w_in, i, cs_odd)
            o = _dilated_attention(qkv, n_heads=n_heads)
            xs = _out_proj(o, 0, o, 1, w_out, i, xs)
        kv = _norm_proj(mem2, mem_kv_norm[i], wkv, i, cs_kv)
        xs = _mem_xattn(xs, mem_q_norm[i], wq, kv, wo, i)
        last = i == depth - 1
        xs = _ffn(xs, ffn2_norm[i], w2g, w2u, w2d, i, final_g=final_norm if last else None)
    return xs.reshape(batch, seq, d_model)
```

```python
import functools
import math

import numpy as np
import jax
import jax.numpy as jnp
from jax import lax
from jax.experimental import pallas as pl
from jax.experimental.pallas import tpu as pltpu

F32 = jnp.float32
BF16 = jnp.bfloat16

EPS = 1e-6
NEG_INF = -1e30
LOG2E = math.log2(math.e)
HEAD_DIM = 128
GRID_W = 64
NA_KH = 8
NA_KW = 16
N_HEADS_MEM = 4
DILATED_PAIRS = ((128, 1), (512, 4), (2048, 16))

VMEM_LIMIT = 52 * 1024 * 1024

_NT = (((1,), (1,)), ((), ()))


def _params(*sem):
    return pltpu.CompilerParams(dimension_semantics=sem, vmem_limit_bytes=VMEM_LIMIT)


def _rms(x, g):
    return x * lax.rsqrt(jnp.mean(x * x, axis=-1, keepdims=True) + EPS) * g


def _alibi_slopes(n):
    return np.exp2(-8.0 * np.arange(1, n + 1) / n).astype(np.float32)


def _layer_spec(layer, rows, cols, index_map):
    return pl.BlockSpec((None, rows, cols), lambda *g: (layer,) + tuple(index_map(*g)))


def _col_blocked(w, tn):
    depth, k, n = w.shape
    return w.astype(BF16).reshape(depth, k, n // tn, tn).transpose(0, 2, 1, 3)


def _col_block_spec(layer, rows, tn, index_map):
    def imap(*g):
        r, j = index_map(*g)
        return (layer, j, r, 0)
    return pl.BlockSpec((None, None, rows, tn), imap)


def _ffn_body(x_ref, g_ref, wg_ref, wu_ref, wd_ref, fg_ref, o_ref, h_ref, *, final_norm):
    j = pl.program_id(1)

    @pl.when(j == 0)
    def _():
        x = x_ref[...]
        h_ref[...] = _rms(x, g_ref[...]).astype(BF16)
        o_ref[...] = x

    h = h_ref[...]
    a = jnp.dot(h, wg_ref[...], preferred_element_type=F32)
    b = jnp.dot(h, wu_ref[...], preferred_element_type=F32)
    act = (0.5 * a * jax.nn.sigmoid(a)) * b
    o_ref[...] += jnp.dot(act.astype(BF16), wd_ref[...], preferred_element_type=F32)

    if final_norm:
        @pl.when(j == pl.num_programs(1) - 1)
        def _():
            o_ref[...] = _rms(o_ref[...], fg_ref[...])


FFN_TF = 512
PROJ_TN = 512


def _ffn(x, g, wg, wu, wd, layer, final_g=None, *, tm=512):
    s, d = x.shape
    tf = FFN_TF
    f = wd.shape[1]
    final_norm = final_g is not None
    fg = final_g if final_norm else g
    return pl.pallas_call(
        functools.partial(_ffn_body, final_norm=final_norm),
        grid=(s // tm, f // tf),
        in_specs=[
            pl.BlockSpec((tm, d), lambda i, j: (i, 0)),
            pl.BlockSpec((1, d), lambda i, j: (0, 0)),
            _col_block_spec(layer, d, tf, lambda i, j: (0, j)),
            _col_block_spec(layer, d, tf, lambda i, j: (0, j)),
            _layer_spec(layer, tf, d, lambda i, j: (j, 0)),
            pl.BlockSpec((1, d), lambda i, j: (0, 0)),
        ],
        out_specs=pl.BlockSpec((tm, d), lambda i, j: (i, 0)),
        out_shape=jax.ShapeDtypeStruct((s, d), F32),
        scratch_shapes=[pltpu.VMEM((tm, d), BF16)],
        compiler_params=_params("parallel", "arbitrary"),
        name="ffn",
    )(x, g.reshape(1, d), wg, wu, wd, fg.reshape(1, d))


def _norm_proj_body(x_ref, g_ref, w_ref, cs_ref, o_ref, h_ref):
    @pl.when(pl.program_id(1) == 0)
    def _():
        h_ref[...] = _rms(x_ref[...], g_ref[...]).astype(BF16)

    acc = jnp.dot(h_ref[...], w_ref[...], preferred_element_type=F32)
    o_ref[...] = (acc * cs_ref[...]).astype(o_ref.dtype)


def _norm_proj(x, g, w, layer, col_scale, *, tm=1024):
    s, d = x.shape
    tn = PROJ_TN
    n = w.shape[1] * tn
    tm = min(tm, s)
    return pl.pallas_call(
        _norm_proj_body,
        grid=(s // tm, n // tn),
        in_specs=[
            pl.BlockSpec((tm, d), lambda i, j: (i, 0)),
            pl.BlockSpec((1, d), lambda i, j: (0, 0)),
            _col_block_spec(layer, d, tn, lambda i, j: (0, j)),
            pl.BlockSpec((1, tn), lambda i, j: (0, j)),
        ],
        out_specs=pl.BlockSpec((tm, tn), lambda i, j: (i, j)),
        out_shape=jax.ShapeDtypeStruct((s, n), BF16),
        scratch_shapes=[pltpu.VMEM((tm, d), BF16)],
        compiler_params=_params("parallel", "arbitrary"),
        name="norm_proj",
    )(x, g.reshape(1, d), w, jnp.asarray(col_scale, F32).reshape(1, n))


def _col_scale(n, scaled):
    cs = np.ones((n,), np.float32)
    for lo, hi, val in scaled:
        cs[lo:hi] = val
    return cs


def _block_start(kb, t):
    return kb * t if isinstance(kb, int) else pl.multiple_of(kb * t, t)


ONES_ROWS = 16


def _values_transposed(src_ref, dst_ref, seq, chunk):
    def body(c, carry):
        r0 = pl.multiple_of(c * chunk, chunk)
        dst_ref[0:HEAD_DIM, pl.ds(r0, chunk)] = src_ref[pl.ds(r0, chunk), :].astype(F32).T.astype(BF16)
        return carry
    lax.fori_loop(0, seq // chunk, body, 0)
    dst_ref[HEAD_DIM:, :] = jnp.ones((ONES_ROWS, seq), BF16)


def _diff_body(slopes_ref, lam_ref, q_ref, k_ref, v_ref, sg_ref, o_ref,
               vt_ref, r_ref, u0_ref, u1_ref, mx0_ref, mx1_ref, acc_ref, *, t, seq, out_scale):
    h = pl.program_id(0)
    i = pl.program_id(1)
    nk = seq // t
    dk = HEAD_DIM // 2
    slope = slopes_ref[h]
    u_refs = (u0_ref, u1_ref)
    mx_refs = (mx0_ref, mx1_ref)

    @pl.when(i == 0)
    def _():
        _values_transposed(v_ref, vt_ref, seq, t)
        rel = (lax.broadcasted_iota(jnp.int32, (t, t), 0)
               - lax.broadcasted_iota(jnp.int32, (t, t), 1)).astype(F32)
        r = slope * rel
        r_ref[0] = r
        r_ref[1] = -r
        r_ref[2] = -jnp.abs(r)

    q = q_ref[...]
    lane = lax.broadcasted_iota(jnp.int32, q.shape, 1)
    zero = jnp.zeros_like(q)
    q_half = (jnp.where(lane < dk, q, zero), jnp.where(lane >= dk, q, zero))
    acc_ref[...] = jnp.zeros_like(acc_ref)

    def scores(kb, slot):
        side = jnp.where(kb < i, 0, jnp.where(kb > i, 1, 2))
        k = k_ref[pl.ds(_block_start(kb, t), t), :]
        r = r_ref[side]
        for c in range(2):
            u = lax.dot_general(k, q_half[c], _NT, preferred_element_type=F32) + r
            u_refs[slot][c] = u
            mx_refs[slot][c] = jnp.max(u, axis=0, keepdims=True)

    def update(kb, slot, carry):
        vt = vt_ref[:, pl.ds(_block_start(kb, t), t)]
        off = slope * (jnp.abs(kb - i) * t).astype(F32)
        new = []
        for c in range(2):
            m_old = carry[c]
            m_new = jnp.maximum(m_old, mx_refs[slot][c] - off)
            alpha = jnp.exp2(m_old - m_new)
            p = jnp.exp2(u_refs[slot][c] - (m_new + off))
            acc_ref[c] = alpha * acc_ref[c] + jnp.dot(vt, p.astype(BF16), preferred_element_type=F32)
            new.append(m_new)
        return tuple(new)

    def pair(j, carry):
        kb = 2 * j
        scores(kb + 1, 1)
        carry = update(kb, 0, carry)
        scores(kb + 2, 0)
        return update(kb + 1, 1, carry)

    m_init = jnp.full((1, t), NEG_INF, F32)
    scores(0, 0)
    carry = lax.fori_loop(0, nk // 2 - 1, pair, (m_init, m_init))
    scores(nk - 1, 1)
    carry = update(nk - 2, 0, carry)
    update(nk - 1, 1, carry)

    lam = lam_ref[0]
    o0 = acc_ref[0, 0:HEAD_DIM] / acc_ref[0, HEAD_DIM:HEAD_DIM + 1]
    o1 = acc_ref[1, 0:HEAD_DIM] / acc_ref[1, HEAD_DIM:HEAD_DIM + 1]
    ot = o0 - lam * o1
    o = _rms(ot.T, sg_ref[...]) * out_scale
    o_ref[...] = o.astype(o_ref.dtype)


def _diff_attention(qkv, lam, subln, lambda_init, *, n_heads, q_col, k_col, v_col, t=512):
    seq = qkv.shape[0]
    assert (seq // t) % 2 == 0
    slopes = jnp.asarray(_alibi_slopes(n_heads) * np.float32(LOG2E))
    smem = pl.BlockSpec(memory_space=pltpu.SMEM)
    tile = pltpu.VMEM((2, t, t), F32)
    colmax = pltpu.VMEM((2, 1, t), F32)
    return pl.pallas_call(
        functools.partial(_diff_body, t=t, seq=seq, out_scale=1.0 - lambda_init),
        grid=(n_heads, seq // t),
        in_specs=[
            smem,
            smem,
            pl.BlockSpec((t, HEAD_DIM), lambda h, i: (i, q_col + h)),
            pl.BlockSpec((seq, HEAD_DIM), lambda h, i: (0, k_col + h)),
            pl.BlockSpec((seq, HEAD_DIM), lambda h, i: (0, v_col + h)),
            pl.BlockSpec((1, HEAD_DIM), lambda h, i: (0, 0)),
        ],
        out_specs=pl.BlockSpec((t, HEAD_DIM), lambda h, i: (i, h)),
        out_shape=jax.ShapeDtypeStruct((seq, n_heads * HEAD_DIM), BF16),
        scratch_shapes=[pltpu.VMEM((HEAD_DIM + ONES_ROWS, seq), BF16), pltpu.VMEM((3, t, t), F32),
                        tile, tile, colmax, colmax, pltpu.VMEM((2, HEAD_DIM + ONES_ROWS, t), F32)],
        compiler_params=_params("parallel", "arbitrary"),
        name="diff_attn",
    )(slopes, lam.reshape(1).astype(F32), qkv, qkv, qkv, subln.reshape(1, HEAD_DIM))


NA_ROWS_Q = 4
NA_ROWS_K = 12


def _na_bias_table(rpb, rows):
    kh, kw, w = NA_KH, NA_KW, GRID_W
    n_heads = rpb.shape[0]
    nblk = rows // NA_ROWS_Q
    rpb = rpb.astype(F32) * LOG2E
    ext_idx = np.clip(np.arange(2 * w - 1) - (w - 1) + kw - 1, 0, 2 * kw - 2)
    lo = int(np.argmax(ext_idx > 0))
    hi = int(np.argmax(ext_idx == 2 * kw - 2))
    ext = jnp.concatenate([jnp.repeat(rpb[..., :1], lo - 1, axis=-1), rpb,
                           jnp.repeat(rpb[..., -1:], 2 * w - 1 - hi - 1, axis=-1)], axis=-1)
    tab = jnp.stack([ext[..., w - 1 - qc:2 * w - 1 - qc] for qc in range(w)], axis=-2)
    c = np.arange(w)
    c0 = np.clip(c - kw // 2, 0, w - kw)
    col_ok = (c[None, :] >= c0[:, None]) & (c[None, :] < c0[:, None] + kw)
    tab = jnp.where(col_ok, tab, NEG_INF)
    masked = jnp.full((n_heads, w, w), NEG_INF, F32)
    cases = []
    for blk in (0, 1, nblk - 1):
        ws = int(np.clip(NA_ROWS_Q * blk - kh // 2, 0, rows - NA_ROWS_K))
        q_rows = []
        for a in range(NA_ROWS_Q):
            r = NA_ROWS_Q * blk + a
            r0 = int(np.clip(r - kh // 2, 0, rows - kh))
            tiles = []
            for wi in range(NA_ROWS_K):
                kr = ws + wi
                tiles.append(tab[:, kr - r + kh - 1] if r0 <= kr < r0 + kh else masked)
            q_rows.append(jnp.concatenate(tiles, axis=-1))
        cases.append(jnp.concatenate(q_rows, axis=-2))
    return jnp.stack(cases, axis=1)


def _na_body(q_ref, k_ref, v_ref, b_ref, o_ref, *, rows):
    rb = pl.program_id(1)
    ws = jnp.clip(NA_ROWS_Q * rb - NA_KH // 2, 0, rows - NA_ROWS_K)
    start = pl.multiple_of(ws * GRID_W, GRID_W)
    nkeys = NA_ROWS_K * GRID_W
    k = k_ref[pl.ds(start, nkeys), :]
    v = v_ref[pl.ds(start, nkeys), :]
    s = lax.dot_general(q_ref[...], k, _NT, preferred_element_type=F32) + b_ref[0, 0]
    m = jnp.max(s, axis=-1, keepdims=True)
    p = jnp.exp2(s - m)
    l = jnp.sum(p, axis=-1, keepdims=True)
    o = jnp.dot(p.astype(BF16), v, preferred_element_type=F32) / l
    o_ref[...] = o.astype(o_ref.dtype)


def _neighborhood_attention(qkv, rpb, *, n_heads, q_col, k_col, v_col):
    seq = qkv.shape[0]
    rows = seq // GRID_W
    nblk = rows // NA_ROWS_Q
    tq = NA_ROWS_Q * GRID_W
    bias = _na_bias_table(rpb, rows)

    def case(rb):
        return jnp.where(rb == 0, 0, jnp.where(rb == nblk - 1, 2, 1))

    return pl.pallas_call(
        functools.partial(_na_body, rows=rows),
        grid=(n_heads, nblk),
        in_specs=[
            pl.BlockSpec((tq, HEAD_DIM), lambda h, rb: (rb, q_col + h)),
            pl.BlockSpec((seq, HEAD_DIM), lambda h, rb: (0, k_col + h)),
            pl.BlockSpec((seq, HEAD_DIM), lambda h, rb: (0, v_col + h)),
            pl.BlockSpec((1, 1, tq, NA_ROWS_K * GRID_W), lambda h, rb: (h, case(rb), 0, 0)),
        ],
        out_specs=pl.BlockSpec((tq, HEAD_DIM), lambda h, rb: (rb, h)),
        out_shape=jax.ShapeDtypeStruct((seq, n_heads * HEAD_DIM), BF16),
        compiler_params=_params("parallel", "arbitrary"),
        name="na_attn",
    )(qkv, qkv, qkv, bias)


def _dil_body(slopes_ref, q_ref, k_ref, v_ref, o_ref, vt_ref, b_ref, u0_ref, u1_ref, acc_ref,
              *, t, seq, nb):
    h = pl.program_id(0)
    i = pl.program_id(1)
    nk = seq // t
    nsteps = 2 * nb + 1
    slope = slopes_ref[h]
    u_refs = (u0_ref, u1_ref)

    @pl.when(i == 0)
    def _():
        _values_transposed(v_ref, vt_ref, seq, t)
        rel = (lax.broadcasted_iota(jnp.int32, (t, t), 0)
               - lax.broadcasted_iota(jnp.int32, (t, t), 1))
        for dlt in range(-nb, nb + 1):
            ad = jnp.abs(rel + dlt * t)
            cnt = jnp.zeros((t, t), jnp.int32)
            for window, dil in DILATED_PAIRS:
                ok = (ad <= window // 2) & ((ad & (dil - 1)) == 0)
                cnt = cnt + jnp.where(ok, 1, 0)
            logc = jnp.where(cnt == 3, math.log2(3.0), jnp.where(cnt == 2, 1.0, 0.0))
            b_ref[dlt + nb] = jnp.where(cnt > 0, logc - slope * ad.astype(F32), NEG_INF)
        b_ref[nsteps] = jnp.full((t, t), NEG_INF, F32)

    q = q_ref[...]
    acc_ref[...] = jnp.zeros_like(acc_ref)

    def scores(step, slot):
        kb = i - nb + step
        inside = (kb >= 0) & (kb < nk)
        k = k_ref[pl.ds(_block_start(jnp.clip(kb, 0, nk - 1), t), t), :]
        u = lax.dot_general(k, q, _NT, preferred_element_type=F32) + b_ref[jnp.where(inside, step, nsteps)]
        u_refs[slot][...] = u
        return jnp.max(u, axis=0, keepdims=True)

    def update(step, slot, mx, m_old):
        kb = jnp.clip(i - nb + step, 0, nk - 1)
        vt = vt_ref[:, pl.ds(_block_start(kb, t), t)]
        m_new = jnp.maximum(m_old, mx)
        alpha = jnp.exp2(m_old - m_new)
        p = jnp.exp2(u_refs[slot][...] - m_new)
        acc_ref[...] = alpha * acc_ref[...] + jnp.dot(vt, p.astype(BF16), preferred_element_type=F32)
        return m_new

    m = jnp.full((1, t), NEG_INF, F32)
    mx = scores(0, 0)
    for step in range(nsteps):
        mx_next = scores(step + 1, (step + 1) % 2) if step + 1 < nsteps else None
        m = update(step, step % 2, mx, m)
        mx = mx_next
    o_ref[...] = (acc_ref[0:HEAD_DIM] / acc_ref[HEAD_DIM:HEAD_DIM + 1]).T.astype(o_ref.dtype)


def _dilated_attention(qkv, *, n_heads, t=512):
    seq = qkv.shape[0]
    for window, dil in DILATED_PAIRS:
        assert dil & (dil - 1) == 0 and window % (2 * dil) == 0
    reach = max(window // 2 for window, _ in DILATED_PAIRS)
    nb = (reach - 1) // t + 1
    slopes = jnp.asarray(_alibi_slopes(n_heads) * np.float32(LOG2E))
    return pl.pallas_call(
        functools.partial(_dil_body, t=t, seq=seq, nb=nb),
        grid=(n_heads, seq // t),
        in_specs=[
            pl.BlockSpec(memory_space=pltpu.SMEM),
            pl.BlockSpec((t, HEAD_DIM), lambda h, i: (i, h)),
            pl.BlockSpec((seq, HEAD_DIM), lambda h, i: (0, n_heads + h)),
            pl.BlockSpec((seq, HEAD_DIM), lambda h, i: (0, 2 * n_heads + h)),
        ],
        out_specs=pl.BlockSpec((t, HEAD_DIM), lambda h, i: (i, h)),
        out_shape=jax.ShapeDtypeStruct((seq, n_heads * HEAD_DIM), BF16),
        scratch_shapes=[pltpu.VMEM((HEAD_DIM + ONES_ROWS, seq), BF16), pltpu.VMEM((2 * nb + 2, t, t), F32),
                        pltpu.VMEM((t, t), F32), pltpu.VMEM((t, t), F32),
                        pltpu.VMEM((HEAD_DIM + ONES_ROWS, t), F32)],
        compiler_params=_params("parallel", "arbitrary"),
        name="dilated_attn",
    )(slopes, qkv, qkv, qkv)


def _out_proj_body(a_ref, b_ref, wa_ref, wb_ref, x_ref, o_ref):
    acc = jnp.dot(a_ref[...], wa_ref[...], preferred_element_type=F32)
    acc += jnp.dot(b_ref[...], wb_ref[...], preferred_element_type=F32)
    o_ref[...] = x_ref[...] + acc


def _out_proj(a, a_col, b, b_col, w, layer, x, *, tm=512):
    s, d = x.shape
    tn = PROJ_TN
    kh = w.shape[2] // 2
    return pl.pallas_call(
        _out_proj_body,
        grid=(s // tm, d // tn),
        in_specs=[
            pl.BlockSpec((tm, kh), lambda i, j: (i, a_col)),
            pl.BlockSpec((tm, kh), lambda i, j: (i, b_col)),
            _col_block_spec(layer, kh, tn, lambda i, j: (0, j)),
            _col_block_spec(layer, kh, tn, lambda i, j: (1, j)),
            pl.BlockSpec((tm, tn), lambda i, j: (i, j)),
        ],
        out_specs=pl.BlockSpec((tm, tn), lambda i, j: (i, j)),
        out_shape=jax.ShapeDtypeStruct((s, d), F32),
        compiler_params=_params("parallel", "arbitrary"),
        name="out_proj",
    )(a, b, w, w, x)


def _mem_body(x_ref, g_ref, wq_ref, kv_ref, wo_ref, o_ref, *, scale):
    x = x_ref[...]
    h = _rms(x, g_ref[...]).astype(BF16)
    q = jnp.dot(h, wq_ref[...], preferred_element_type=F32).astype(BF16)
    width = N_HEADS_MEM * HEAD_DIM
    outs = []
    for hd in range(N_HEADS_MEM):
        lo, hi = hd * HEAD_DIM, (hd + 1) * HEAD_DIM
        s = lax.dot_general(q[:, lo:hi], kv_ref[:, lo:hi], _NT, preferred_element_type=F32) * scale
        m = jnp.max(s, axis=-1, keepdims=True)
        p = jnp.exp(s - m)
        l = jnp.sum(p, axis=-1, keepdims=True)
        o = jnp.dot(p.astype(BF16), kv_ref[:, width + lo:width + hi], preferred_element_type=F32) / l
        outs.append(o.astype(BF16))
    o = jnp.concatenate(outs, axis=-1)
    o_ref[...] = x + jnp.dot(o, wo_ref[...], preferred_element_type=F32)


def _mem_xattn(x, g, wq, kv, wo, layer, *, tm=512):
    s, d = x.shape
    n_mem, kvw = kv.shape
    width = wq.shape[2]
    return pl.pallas_call(
        functools.partial(_mem_body, scale=HEAD_DIM ** -0.5),
        grid=(s // tm,),
        in_specs=[
            pl.BlockSpec((tm, d), lambda i: (i, 0)),
            pl.BlockSpec((1, d), lambda i: (0, 0)),
            _layer_spec(layer, d, width, lambda i: (0, 0)),
            pl.BlockSpec((n_mem, kvw), lambda i: (0, 0)),
            _layer_spec(layer, width, d, lambda i: (0, 0)),
        ],
        out_specs=pl.BlockSpec((tm, d), lambda i: (i, 0)),
        out_shape=jax.ShapeDtypeStruct((s, d), F32),
        compiler_params=_params("parallel"),
        name="mem_xattn",
    )(x, g.reshape(1, d), wq, kv, wo)


def kernel(x, mem, ffn1_norm, ffn1_w_gate, ffn1_w_up, ffn1_w_down, mix_norm, mix_w_in, mix_w_out, diff_lq1, diff_lk1, diff_lq2, diff_lk2, diff_subln, na_rpb, mem_q_norm, mem_kv_norm, mem_wq, mem_wkv, mem_wo, ffn2_norm, ffn2_w_gate, ffn2_w_up, ffn2_w_down, final_norm):
    batch, seq, d_model = x.shape
    assert batch == 1
    depth = ffn1_norm.shape[0]
    n_heads = d_model // HEAD_DIM
    in_proj = mix_w_in.shape[2]
    xs = x.reshape(seq, d_model)
    mem2 = mem.reshape(mem.shape[1], d_model)

    bf = lambda t: t.astype(BF16)
    w1g, w1u, w1d = _col_blocked(ffn1_w_gate, FFN_TF), _col_blocked(ffn1_w_up, FFN_TF), bf(ffn1_w_down)
    w2g, w2u, w2d = _col_blocked(ffn2_w_gate, FFN_TF), _col_blocked(ffn2_w_up, FFN_TF), bf(ffn2_w_down)
    w_in, w_out = _col_blocked(mix_w_in, PROJ_TN), _col_blocked(mix_w_out, PROJ_TN)
    wq, wkv, wo = bf(mem_wq), _col_blocked(mem_wkv, PROJ_TN), bf(mem_wo)

    hh = n_heads // 2
    half_w = hh * HEAD_DIM
    cs_even = _col_scale(in_proj, [(0, half_w, (HEAD_DIM // 2) ** -0.5 * LOG2E),
                                   (3 * half_w, 4 * half_w, HEAD_DIM ** -0.5 * LOG2E)])
    cs_odd = _col_scale(in_proj, [(0, d_model, HEAD_DIM ** -0.5 * LOG2E)])
    cs_kv = _col_scale(mem_wkv.shape[2], [])

    for i in range(depth):
        xs = _ffn(xs, ffn1_norm[i], w1g, w1u, w1d, i)
        if i % 2 == 0:
            e = i // 2
            qkv = _norm_proj(xs, mix_norm[i], w_in, i, cs_even)
            lambda_init = 0.8 - 0.6 * math.exp(-0.3 * i)
            lam = (jnp.exp(jnp.sum(diff_lq1[e].astype(F32) * diff_lk1[e].astype(F32)))
                   - jnp.exp(jnp.sum(diff_lq2[e].astype(F32) * diff_lk2[e].astype(F32)))
                   + lambda_init)
            oa = _diff_attention(qkv, lam, diff_subln[e], lambda_init, n_heads=hh,
                                 q_col=0, k_col=hh, v_col=2 * hh)
            ob = _neighborhood_attention(qkv, na_rpb[e], n_heads=hh,
                                         q_col=3 * hh, k_col=4 * hh, v_col=5 * hh)
            xs = _out_proj(oa, 0, ob, 0, w_out, i, xs)
        else:
            qkv = _norm_proj(xs, mix_norm[i], w_in, i, cs_odd)
            o = _dilated_attention(qkv, n_heads=n_heads)
            xs = _out_proj(o, 0, o, 1, w_out, i, xs)
        kv = _norm_proj(mem2, mem_kv_norm[i], wkv, i, cs_kv)
        xs = _mem_xattn(xs, mem_q_norm[i], wq, kv, wo, i)
        last = i == depth - 1
        xs = _ffn(xs, ffn2_norm[i], w2g, w2u, w2d, i, final_g=final_norm if last else None)
    return xs.reshape(batch, seq, d_model)
```

```python
import functools
import math

import numpy as np
import jax
import jax.numpy as jnp
from jax import lax
from jax.experimental import pallas as pl
from jax.experimental.pallas import tpu as pltpu

F32 = jnp.float32
BF16 = jnp.bfloat16

EPS = 1e-6
NEG_INF = -1e30
LOG2E = math.log2(math.e)
HEAD_DIM = 128
GRID_W = 64
NA_KH = 8
NA_KW = 16
N_HEADS_MEM = 4
DILATED_PAIRS = ((128, 1), (512, 4), (2048, 16))

VMEM_LIMIT = 52 * 1024 * 1024

_NT = (((1,), (1,)), ((), ()))


def _params(*sem):
    return pltpu.CompilerParams(dimension_semantics=sem, vmem_limit_bytes=VMEM_LIMIT)


def _rms(x, g):
    return x * lax.rsqrt(jnp.mean(x * x, axis=-1, keepdims=True) + EPS) * g


def _alibi_slopes(n):
    return np.exp2(-8.0 * np.arange(1, n + 1) / n).astype(np.float32)


def _layer_spec(layer, rows, cols, index_map):
    return pl.BlockSpec((None, rows, cols), lambda *g: (layer,) + tuple(index_map(*g)))


def _ffn_body(x_ref, g_ref, wg_ref, wu_ref, wd_ref, fg_ref, o_ref, h_ref, *, final_norm):
    j = pl.program_id(1)

    @pl.when(j == 0)
    def _():
        x = x_ref[...]
        h_ref[...] = _rms(x, g_ref[...]).astype(BF16)
        o_ref[...] = x

    h = h_ref[...]
    a = jnp.dot(h, wg_ref[...], preferred_element_type=F32)
    b = jnp.dot(h, wu_ref[...], preferred_element_type=F32)
    act = (0.5 * a * jax.nn.sigmoid(a)) * b
    o_ref[...] += jnp.dot(act.astype(BF16), wd_ref[...], preferred_element_type=F32)

    if final_norm:
        @pl.when(j == pl.num_programs(1) - 1)
        def _():
            o_ref[...] = _rms(o_ref[...], fg_ref[...])


FFN_TF = 512
PROJ_TN = 512


def _ffn(x, g, wg, wu, wd, layer, final_g=None, *, tm=512):
    s, d = x.shape
    tf = FFN_TF
    f = wd.shape[1]
    final_norm = final_g is not None
    fg = final_g if final_norm else g
    return pl.pallas_call(
        functools.partial(_ffn_body, final_norm=final_norm),
        grid=(s // tm, f // tf),
        in_specs=[
            pl.BlockSpec((tm, d), lambda i, j: (i, 0)),
            pl.BlockSpec((1, d), lambda i, j: (0, 0)),
            _layer_spec(layer, d, tf, lambda i, j: (0, j)),
            _layer_spec(layer, d, tf, lambda i, j: (0, j)),
            _layer_spec(layer, tf, d, lambda i, j: (j, 0)),
            pl.BlockSpec((1, d), lambda i, j: (0, 0)),
        ],
        out_specs=pl.BlockSpec((tm, d), lambda i, j: (i, 0)),
        out_shape=jax.ShapeDtypeStruct((s, d), F32),
        scratch_shapes=[pltpu.VMEM((tm, d), BF16)],
        compiler_params=_params("parallel", "arbitrary"),
        name="ffn",
    )(x, g.reshape(1, d), wg, wu, wd, fg.reshape(1, d))


def _norm_proj_body(x_ref, g_ref, w_ref, cs_ref, o_ref, h_ref):
    @pl.when(pl.program_id(1) == 0)
    def _():
        h_ref[...] = _rms(x_ref[...], g_ref[...]).astype(BF16)

    acc = jnp.dot(h_ref[...], w_ref[...], preferred_element_type=F32)
    o_ref[...] = (acc * cs_ref[...]).astype(o_ref.dtype)


def _norm_proj(x, g, w, layer, col_scale, *, tm=1024):
    s, d = x.shape
    tn = PROJ_TN
    n = w.shape[2]
    tm = min(tm, s)
    return pl.pallas_call(
        _norm_proj_body,
        grid=(s // tm, n // tn),
        in_specs=[
            pl.BlockSpec((tm, d), lambda i, j: (i, 0)),
            pl.BlockSpec((1, d), lambda i, j: (0, 0)),
            _layer_spec(layer, d, tn, lambda i, j: (0, j)),
            pl.BlockSpec((1, tn), lambda i, j: (0, j)),
        ],
        out_specs=pl.BlockSpec((tm, tn), lambda i, j: (i, j)),
        out_shape=jax.ShapeDtypeStruct((s, n), BF16),
        scratch_shapes=[pltpu.VMEM((tm, d), BF16)],
        compiler_params=_params("parallel", "arbitrary"),
        name="norm_proj",
    )(x, g.reshape(1, d), w, jnp.asarray(col_scale, F32).reshape(1, n))


def _col_scale(n, scaled):
    cs = np.ones((n,), np.float32)
    for lo, hi, val in scaled:
        cs[lo:hi] = val
    return cs


def _block_start(kb, t):
    return kb * t if isinstance(kb, int) else pl.multiple_of(kb * t, t)


ONES_ROWS = 16


def _values_transposed(src_ref, dst_ref, seq, chunk):
    def body(c, carry):
        r0 = pl.multiple_of(c * chunk, chunk)
        dst_ref[0:HEAD_DIM, pl.ds(r0, chunk)] = src_ref[pl.ds(r0, chunk), :].astype(F32).T.astype(BF16)
        return carry
    lax.fori_loop(0, seq // chunk, body, 0)
    dst_ref[HEAD_DIM:, :] = jnp.ones((ONES_ROWS, seq), BF16)


def _diff_body(slopes_ref, lam_ref, q_ref, k_ref, v_ref, sg_ref, o_ref,
               vt_ref, r_ref, u0_ref, u1_ref, mx0_ref, mx1_ref, acc_ref, *, t, seq, out_scale):
    h = pl.program_id(0)
    i = pl.program_id(1)
    nk = seq // t
    dk = HEAD_DIM // 2
    slope = slopes_ref[h]
    u_refs = (u0_ref, u1_ref)
    mx_refs = (mx0_ref, mx1_ref)

    @pl.when(i == 0)
    def _():
        _values_transposed(v_ref, vt_ref, seq, t)
        rel = (lax.broadcasted_iota(jnp.int32, (t, t), 0)
               - lax.broadcasted_iota(jnp.int32, (t, t), 1)).astype(F32)
        r = slope * rel
        r_ref[0] = r
        r_ref[1] = -r
        r_ref[2] = -jnp.abs(r)

    q = q_ref[...]
    lane = lax.broadcasted_iota(jnp.int32, q.shape, 1)
    zero = jnp.zeros_like(q)
    q_half = (jnp.where(lane < dk, q, zero), jnp.where(lane >= dk, q, zero))
    acc_ref[...] = jnp.zeros_like(acc_ref)

    def scores(kb, slot):
        side = jnp.where(kb < i, 0, jnp.where(kb > i, 1, 2))
        k = k_ref[pl.ds(_block_start(kb, t), t), :]
        r = r_ref[side]
        for c in range(2):
            u = lax.dot_general(k, q_half[c], _NT, preferred_element_type=F32) + r
            u_refs[slot][c] = u
            mx_refs[slot][c] = jnp.max(u, axis=0, keepdims=True)

    def update(kb, slot, carry):
        vt = vt_ref[:, pl.ds(_block_start(kb, t), t)]
        off = slope * (jnp.abs(kb - i) * t).astype(F32)
        new = []
        for c in range(2):
            m_old = carry[c]
            m_new = jnp.maximum(m_old, mx_refs[slot][c] - off)
            alpha = jnp.exp2(m_old - m_new)
            p = jnp.exp2(u_refs[slot][c] - (m_new + off))
            acc_ref[c] = alpha * acc_ref[c] + jnp.dot(vt, p.astype(BF16), preferred_element_type=F32)
            new.append(m_new)
        return tuple(new)

    def pair(j, carry):
        kb = 2 * j
        scores(kb + 1, 1)
        carry = update(kb, 0, carry)
        scores(kb + 2, 0)
        return update(kb + 1, 1, carry)

    m_init = jnp.full((1, t), NEG_INF, F32)
    scores(0, 0)
    carry = lax.fori_loop(0, nk // 2 - 1, pair, (m_init, m_init))
    scores(nk - 1, 1)
    carry = update(nk - 2, 0, carry)
    update(nk - 1, 1, carry)

    lam = lam_ref[0]
    o0 = acc_ref[0, 0:HEAD_DIM] / acc_ref[0, HEAD_DIM:HEAD_DIM + 1]
    o1 = acc_ref[1, 0:HEAD_DIM] / acc_ref[1, HEAD_DIM:HEAD_DIM + 1]
    ot = o0 - lam * o1
    o = _rms(ot.T, sg_ref[...]) * out_scale
    o_ref[...] = o.astype(o_ref.dtype)


def _diff_attention(qkv, lam, subln, lambda_init, *, n_heads, q_col, k_col, v_col, t=512):
    seq = qkv.shape[0]
    assert (seq // t) % 2 == 0
    slopes = jnp.asarray(_alibi_slopes(n_heads) * np.float32(LOG2E))
    smem = pl.BlockSpec(memory_space=pltpu.SMEM)
    tile = pltpu.VMEM((2, t, t), F32)
    colmax = pltpu.VMEM((2, 1, t), F32)
    return pl.pallas_call(
        functools.partial(_diff_body, t=t, seq=seq, out_scale=1.0 - lambda_init),
        grid=(n_heads, seq // t),
        in_specs=[
            smem,
            smem,
            pl.BlockSpec((t, HEAD_DIM), lambda h, i: (i, q_col + h)),
            pl.BlockSpec((seq, HEAD_DIM), lambda h, i: (0, k_col + h)),
            pl.BlockSpec((seq, HEAD_DIM), lambda h, i: (0, v_col + h)),
            pl.BlockSpec((1, HEAD_DIM), lambda h, i: (0, 0)),
        ],
        out_specs=pl.BlockSpec((t, HEAD_DIM), lambda h, i: (i, h)),
        out_shape=jax.ShapeDtypeStruct((seq, n_heads * HEAD_DIM), BF16),
        scratch_shapes=[pltpu.VMEM((HEAD_DIM + ONES_ROWS, seq), BF16), pltpu.VMEM((3, t, t), F32),
                        tile, tile, colmax, colmax, pltpu.VMEM((2, HEAD_DIM + ONES_ROWS, t), F32)],
        compiler_params=_params("parallel", "arbitrary"),
        name="diff_attn",
    )(slopes, lam.reshape(1).astype(F32), qkv, qkv, qkv, subln.reshape(1, HEAD_DIM))


NA_ROWS_Q = 4
NA_ROWS_K = 12


def _na_bias_table(rpb, rows):
    kh, kw, w = NA_KH, NA_KW, GRID_W
    n_heads = rpb.shape[0]
    nblk = rows // NA_ROWS_Q
    rpb = rpb.astype(F32) * LOG2E
    ext_idx = np.clip(np.arange(2 * w - 1) - (w - 1) + kw - 1, 0, 2 * kw - 2)
    lo = int(np.argmax(ext_idx > 0))
    hi = int(np.argmax(ext_idx == 2 * kw - 2))
    ext = jnp.concatenate([jnp.repeat(rpb[..., :1], lo - 1, axis=-1), rpb,
                           jnp.repeat(rpb[..., -1:], 2 * w - 1 - hi - 1, axis=-1)], axis=-1)
    tab = jnp.stack([ext[..., w - 1 - qc:2 * w - 1 - qc] for qc in range(w)], axis=-2)
    c = np.arange(w)
    c0 = np.clip(c - kw // 2, 0, w - kw)
    col_ok = (c[None, :] >= c0[:, None]) & (c[None, :] < c0[:, None] + kw)
    tab = jnp.where(col_ok, tab, NEG_INF)
    masked = jnp.full((n_heads, w, w), NEG_INF, F32)
    cases = []
    for blk in (0, 1, nblk - 1):
        ws = int(np.clip(NA_ROWS_Q * blk - kh // 2, 0, rows - NA_ROWS_K))
        q_rows = []
        for a in range(NA_ROWS_Q):
            r = NA_ROWS_Q * blk + a
            r0 = int(np.clip(r - kh // 2, 0, rows - kh))
            tiles = []
            for wi in range(NA_ROWS_K):
                kr = ws + wi
                tiles.append(tab[:, kr - r + kh - 1] if r0 <= kr < r0 + kh else masked)
            q_rows.append(jnp.concatenate(tiles, axis=-1))
        cases.append(jnp.concatenate(q_rows, axis=-2))
    return jnp.stack(cases, axis=1).swapaxes(-1, -2)


def _na_body(q_ref, k_ref, v_ref, b_ref, o_ref, vt_ref, u0_ref, u1_ref, *, rows):
    seq = rows * GRID_W
    nblk = rows // NA_ROWS_Q
    tq = NA_ROWS_Q * GRID_W
    nkeys = NA_ROWS_K * GRID_W
    u_refs = (u0_ref, u1_ref)
    _values_transposed(v_ref, vt_ref, seq, 512)

    def window(rb):
        ws = jnp.clip(NA_ROWS_Q * rb - NA_KH // 2, 0, rows - NA_ROWS_K)
        return pl.multiple_of(ws * GRID_W, NA_ROWS_Q * GRID_W)

    def scores(rb, slot):
        case = jnp.where(rb == 0, 0, jnp.where(rb == nblk - 1, 2, 1))
        k = k_ref[pl.ds(window(rb), nkeys), :]
        q = q_ref[pl.ds(_block_start(rb, tq), tq), :]
        u_refs[slot][...] = lax.dot_general(k, q, _NT, preferred_element_type=F32) + b_ref[case]

    def finish(rb, slot):
        u = u_refs[slot][...]
        p = jnp.exp2(u - jnp.max(u, axis=0, keepdims=True))
        vt = vt_ref[:, pl.ds(window(rb), nkeys)]
        acc = jnp.dot(vt, p.astype(BF16), preferred_element_type=F32)
        o = (acc[0:HEAD_DIM] / acc[HEAD_DIM:HEAD_DIM + 1]).T
        o_ref[pl.ds(_block_start(rb, tq), tq), :] = o.astype(o_ref.dtype)

    def pair(j, carry):
        rb = 2 * j
        scores(rb + 1, 1)
        finish(rb, 0)
        scores(rb + 2, 0)
        finish(rb + 1, 1)
        return carry

    scores(0, 0)
    lax.fori_loop(0, nblk // 2 - 1, pair, 0)
    scores(nblk - 1, 1)
    finish(nblk - 2, 0)
    finish(nblk - 1, 1)


def _neighborhood_attention(qkv, rpb, *, n_heads, q_col, k_col, v_col):
    seq = qkv.shape[0]
    rows = seq // GRID_W
    assert (rows // NA_ROWS_Q) % 2 == 0 and NA_ROWS_Q == NA_KH // 2
    tq = NA_ROWS_Q * GRID_W
    nkeys = NA_ROWS_K * GRID_W
    bias = _na_bias_table(rpb, rows)
    head_cols = lambda col: pl.BlockSpec((seq, HEAD_DIM), lambda h: (0, col + h))
    return pl.pallas_call(
        functools.partial(_na_body, rows=rows),
        grid=(n_heads,),
        in_specs=[
            head_cols(q_col),
            head_cols(k_col),
            head_cols(v_col),
            pl.BlockSpec((None, 3, nkeys, tq), lambda h: (h, 0, 0, 0)),
        ],
        out_specs=head_cols(0),
        out_shape=jax.ShapeDtypeStruct((seq, n_heads * HEAD_DIM), BF16),
        scratch_shapes=[pltpu.VMEM((HEAD_DIM + ONES_ROWS, seq), BF16),
                        pltpu.VMEM((nkeys, tq), F32), pltpu.VMEM((nkeys, tq), F32)],
        compiler_params=_params("arbitrary"),
        name="na_attn",
    )(qkv, qkv, qkv, bias)


def _dil_body(slopes_ref, q_ref, k_ref, v_ref, o_ref, vt_ref, b_ref, u0_ref, u1_ref, acc_ref,
              *, t, seq, nb):
    h = pl.program_id(0)
    i = pl.program_id(1)
    nk = seq // t
    nsteps = 2 * nb + 1
    slope = slopes_ref[h]
    u_refs = (u0_ref, u1_ref)

    @pl.when(i == 0)
    def _():
        _values_transposed(v_ref, vt_ref, seq, t)
        rel = (lax.broadcasted_iota(jnp.int32, (t, t), 0)
               - lax.broadcasted_iota(jnp.int32, (t, t), 1))
        for dlt in range(-nb, nb + 1):
            ad = jnp.abs(rel + dlt * t)
            cnt = jnp.zeros((t, t), jnp.int32)
            for window, dil in DILATED_PAIRS:
                ok = (ad <= window // 2) & ((ad & (dil - 1)) == 0)
                cnt = cnt + jnp.where(ok, 1, 0)
            logc = jnp.where(cnt == 3, math.log2(3.0), jnp.where(cnt == 2, 1.0, 0.0))
            b_ref[dlt + nb] = jnp.where(cnt > 0, logc - slope * ad.astype(F32), NEG_INF)
        b_ref[nsteps] = jnp.full((t, t), NEG_INF, F32)

    q = q_ref[...]
    acc_ref[...] = jnp.zeros_like(acc_ref)

    def scores(step, slot):
        kb = i - nb + step
        inside = (kb >= 0) & (kb < nk)
        k = k_ref[pl.ds(_block_start(jnp.clip(kb, 0, nk - 1), t), t), :]
        u = lax.dot_general(k, q, _NT, preferred_element_type=F32) + b_ref[jnp.where(inside, step, nsteps)]
        u_refs[slot][...] = u
        return jnp.max(u, axis=0, keepdims=True)

    def update(step, slot, mx, m_old):
        kb = jnp.clip(i - nb + step, 0, nk - 1)
        vt = vt_ref[:, pl.ds(_block_start(kb, t), t)]
        m_new = jnp.maximum(m_old, mx)
        alpha = jnp.exp2(m_old - m_new)
        p = jnp.exp2(u_refs[slot][...] - m_new)
        acc_ref[...] = alpha * acc_ref[...] + jnp.dot(vt, p.astype(BF16), preferred_element_type=F32)
        return m_new

    m = jnp.full((1, t), NEG_INF, F32)
    mx = scores(0, 0)
    for step in range(nsteps):
        mx_next = scores(step + 1, (step + 1) % 2) if step + 1 < nsteps else None
        m = update(step, step % 2, mx, m)
        mx = mx_next
    o_ref[...] = (acc_ref[0:HEAD_DIM] / acc_ref[HEAD_DIM:HEAD_DIM + 1]).T.astype(o_ref.dtype)


def _dilated_attention(qkv, *, n_heads, t=512):
    seq = qkv.shape[0]
    for window, dil in DILATED_PAIRS:
        assert dil & (dil - 1) == 0 and window % (2 * dil) == 0
    reach = max(window // 2 for window, _ in DILATED_PAIRS)
    nb = (reach - 1) // t + 1
    slopes = jnp.asarray(_alibi_slopes(n_heads) * np.float32(LOG2E))
    return pl.pallas_call(
        functools.partial(_dil_body, t=t, seq=seq, nb=nb),
        grid=(n_heads, seq // t),
        in_specs=[
            pl.BlockSpec(memory_space=pltpu.SMEM),
            pl.BlockSpec((t, HEAD_DIM), lambda h, i: (i, h)),
            pl.BlockSpec((seq, HEAD_DIM), lambda h, i: (0, n_heads + h)),
            pl.BlockSpec((seq, HEAD_DIM), lambda h, i: (0, 2 * n_heads + h)),
        ],
        out_specs=pl.BlockSpec((t, HEAD_DIM), lambda h, i: (i, h)),
        out_shape=jax.ShapeDtypeStruct((seq, n_heads * HEAD_DIM), BF16),
        scratch_shapes=[pltpu.VMEM((HEAD_DIM + ONES_ROWS, seq), BF16), pltpu.VMEM((2 * nb + 2, t, t), F32),
                        pltpu.VMEM((t, t), F32), pltpu.VMEM((t, t), F32),
                        pltpu.VMEM((HEAD_DIM + ONES_ROWS, t), F32)],
        compiler_params=_params("parallel", "arbitrary"),
        name="dilated_attn",
    )(slopes, qkv, qkv, qkv)


def _post_mix_body(a_ref, b_ref, wa_ref, wb_ref, x_ref, g_ref, wq_ref, kv_ref, wo_ref, o_ref, *, scale):
    y = x_ref[...] + jnp.dot(a_ref[...], wa_ref[...], preferred_element_type=F32)
    y = y + jnp.dot(b_ref[...], wb_ref[...], preferred_element_type=F32)
    h = _rms(y, g_ref[...]).astype(BF16)
    q = jnp.dot(h, wq_ref[...], preferred_element_type=F32).astype(BF16)
    width = N_HEADS_MEM * HEAD_DIM
    outs = []
    for hd in range(N_HEADS_MEM):
        lo, hi = hd * HEAD_DIM, (hd + 1) * HEAD_DIM
        s = lax.dot_general(q[:, lo:hi], kv_ref[:, lo:hi], _NT, preferred_element_type=F32) * scale
        m = jnp.max(s, axis=-1, keepdims=True)
        p = jnp.exp(s - m)
        l = jnp.sum(p, axis=-1, keepdims=True)
        o = jnp.dot(p.astype(BF16), kv_ref[:, width + lo:width + hi], preferred_element_type=F32) / l
        outs.append(o.astype(BF16))
    o = jnp.concatenate(outs, axis=-1)
    o_ref[...] = y + jnp.dot(o, wo_ref[...], preferred_element_type=F32)


def _post_mix(a, a_col, b, b_col, w_out, x, g, wq, kv, wo, layer, *, tm=256):
    s, d = x.shape
    kh = w_out.shape[1] // 2
    n_mem, kvw = kv.shape
    width = wq.shape[2]
    once = pl.Buffered(1)

    def resident(rows, cols, r):
        return pl.BlockSpec((None, rows, cols), lambda i: (layer, r, 0), pipeline_mode=once)

    return pl.pallas_call(
        functools.partial(_post_mix_body, scale=HEAD_DIM ** -0.5),
        grid=(s // tm,),
        in_specs=[
            pl.BlockSpec((tm, kh), lambda i: (i, a_col)),
            pl.BlockSpec((tm, kh), lambda i: (i, b_col)),
            resident(kh, d, 0),
            resident(kh, d, 1),
            pl.BlockSpec((tm, d), lambda i: (i, 0)),
            pl.BlockSpec((1, d), lambda i: (0, 0)),
            resident(d, width, 0),
            pl.BlockSpec((n_mem, kvw), lambda i: (0, 0), pipeline_mode=once),
            resident(width, d, 0),
        ],
        out_specs=pl.BlockSpec((tm, d), lambda i: (i, 0)),
        out_shape=jax.ShapeDtypeStruct((s, d), F32),
        compiler_params=_params("parallel"),
        name="post_mix",
    )(a, b, w_out, w_out, x, g.reshape(1, d), wq, kv, wo)


def kernel(x, mem, ffn1_norm, ffn1_w_gate, ffn1_w_up, ffn1_w_down, mix_norm, mix_w_in, mix_w_out, diff_lq1, diff_lk1, diff_lq2, diff_lk2, diff_subln, na_rpb, mem_q_norm, mem_kv_norm, mem_wq, mem_wkv, mem_wo, ffn2_norm, ffn2_w_gate, ffn2_w_up, ffn2_w_down, final_norm):
    batch, seq, d_model = x.shape
    assert batch == 1
    depth = ffn1_norm.shape[0]
    n_heads = d_model // HEAD_DIM
    in_proj = mix_w_in.shape[2]
    xs = x.reshape(seq, d_model)
    mem2 = mem.reshape(mem.shape[1], d_model)

    bf = lambda t: t.astype(BF16)
    w1g, w1u, w1d = bf(ffn1_w_gate), bf(ffn1_w_up), bf(ffn1_w_down)
    w2g, w2u, w2d = bf(ffn2_w_gate), bf(ffn2_w_up), bf(ffn2_w_down)
    w_in, w_out = bf(mix_w_in), bf(mix_w_out)
    wq, wkv, wo = bf(mem_wq), bf(mem_wkv), bf(mem_wo)

    hh = n_heads // 2
    half_w = hh * HEAD_DIM
    cs_even = _col_scale(in_proj, [(0, half_w, (HEAD_DIM // 2) ** -0.5 * LOG2E),
                                   (3 * half_w, 4 * half_w, HEAD_DIM ** -0.5 * LOG2E)])
    cs_odd = _col_scale(in_proj, [(0, d_model, HEAD_DIM ** -0.5 * LOG2E)])
    cs_kv = _col_scale(mem_wkv.shape[2], [])

    for i in range(depth):
        xs = _ffn(xs, ffn1_norm[i], w1g, w1u, w1d, i)
        if i % 2 == 0:
            e = i // 2
            qkv = _norm_proj(xs, mix_norm[i], w_in, i, cs_even)
            lambda_init = 0.8 - 0.6 * math.exp(-0.3 * i)
            lam = (jnp.exp(jnp.sum(diff_lq1[e].astype(F32) * diff_lk1[e].astype(F32)))
                   - jnp.exp(jnp.sum(diff_lq2[e].astype(F32) * diff_lk2[e].astype(F32)))
                   + lambda_init)
            oa = _diff_attention(qkv, lam, diff_subln[e], lambda_init, n_heads=hh,
                                 q_col=0, k_col=hh, v_col=2 * hh)
            ob = _neighborhood_attention(qkv, na_rpb[e], n_heads=hh,
                                         q_col=3 * hh, k_col=4 * hh, v_col=5 * hh)
            mix = (oa, 0, ob, 0)
        else:
            qkv = _norm_proj(xs, mix_norm[i], w_in, i, cs_odd)
            o = _dilated_attention(qkv, n_heads=n_heads)
            mix = (o, 0, o, 1)
        kv = _norm_proj(mem2, mem_kv_norm[i], wkv, i, cs_kv)
        xs = _post_mix(*mix, w_out, xs, mem_q_norm[i], wq, kv, wo, i)
        last = i == depth - 1
        xs = _ffn(xs, ffn2_norm[i], w2g, w2u, w2d, i, final_g=final_norm if last else None)
    return xs.reshape(batch, seq, d_model)
```

```python
import functools
import math

import numpy as np
import jax
import jax.numpy as jnp
from jax import lax
from jax.experimental import pallas as pl
from jax.experimental.pallas import tpu as pltpu

F32 = jnp.float32
BF16 = jnp.bfloat16

EPS = 1e-6
NEG_INF = -1e30
LOG2E = math.log2(math.e)
HEAD_DIM = 128
GRID_W = 64
NA_KH = 8
NA_KW = 16
N_HEADS_MEM = 4
DILATED_PAIRS = ((128, 1), (512, 4), (2048, 16))

VMEM_LIMIT = 52 * 1024 * 1024

_NT = (((1,), (1,)), ((), ()))


def _params(*sem):
    return pltpu.CompilerParams(dimension_semantics=sem, vmem_limit_bytes=VMEM_LIMIT)


def _rms(x, g):
    return x * lax.rsqrt(jnp.mean(x * x, axis=-1, keepdims=True) + EPS) * g


def _alibi_slopes(n):
    return np.exp2(-8.0 * np.arange(1, n + 1) / n).astype(np.float32)


def _layer_spec(layer, rows, cols, index_map):
    return pl.BlockSpec((None, rows, cols), lambda *g: (layer,) + tuple(index_map(*g)))


def _ffn_body(x_ref, g_ref, wg_ref, wu_ref, wd_ref, fg_ref, o_ref, h_ref, *, final_norm):
    j = pl.program_id(1)

    @pl.when(j == 0)
    def _():
        x = x_ref[...]
        h_ref[...] = _rms(x, g_ref[...]).astype(BF16)
        o_ref[...] = x

    h = h_ref[...]
    a = jnp.dot(h, wg_ref[...].astype(BF16), preferred_element_type=F32)
    b = jnp.dot(h, wu_ref[...].astype(BF16), preferred_element_type=F32)
    act = (0.5 * a * jax.nn.sigmoid(a)) * b
    o_ref[...] += jnp.dot(act.astype(BF16), wd_ref[...].astype(BF16), preferred_element_type=F32)

    if final_norm:
        @pl.when(j == pl.num_programs(1) - 1)
        def _():
            o_ref[...] = _rms(o_ref[...], fg_ref[...])


FFN_TF = 256
PROJ_TN = 512


def _ffn(x, g, wg, wu, wd, layer, final_g=None, *, tm=1024):
    s, d = x.shape
    tf = FFN_TF
    f = wd.shape[1]
    final_norm = final_g is not None
    fg = final_g if final_norm else g
    once = pl.Buffered(1)
    return pl.pallas_call(
        functools.partial(_ffn_body, final_norm=final_norm),
        grid=(s // tm, f // tf),
        in_specs=[
            pl.BlockSpec((tm, d), lambda i, j: (i, 0), pipeline_mode=once),
            pl.BlockSpec((1, d), lambda i, j: (0, 0)),
            _layer_spec(layer, d, tf, lambda i, j: (0, j)),
            _layer_spec(layer, d, tf, lambda i, j: (0, j)),
            _layer_spec(layer, tf, d, lambda i, j: (j, 0)),
            pl.BlockSpec((1, d), lambda i, j: (0, 0)),
        ],
        out_specs=pl.BlockSpec((tm, d), lambda i, j: (i, 0)),
        out_shape=jax.ShapeDtypeStruct((s, d), F32),
        scratch_shapes=[pltpu.VMEM((tm, d), BF16)],
        compiler_params=_params("parallel", "arbitrary"),
        name="ffn",
    )(x, g.reshape(1, d), wg, wu, wd, fg.reshape(1, d))


def _norm_proj_body(x_ref, g_ref, w_ref, cs_ref, o_ref, h_ref):
    @pl.when(pl.program_id(1) == 0)
    def _():
        h_ref[...] = _rms(x_ref[...], g_ref[...]).astype(BF16)

    acc = jnp.dot(h_ref[...], w_ref[...], preferred_element_type=F32)
    o_ref[...] = (acc * cs_ref[...]).astype(o_ref.dtype)


def _norm_proj(x, g, w, layer, col_scale, *, tm=1024):
    s, d = x.shape
    tn = PROJ_TN
    n = w.shape[2]
    tm = min(tm, s)
    return pl.pallas_call(
        _norm_proj_body,
        grid=(s // tm, n // tn),
        in_specs=[
            pl.BlockSpec((tm, d), lambda i, j: (i, 0)),
            pl.BlockSpec((1, d), lambda i, j: (0, 0)),
            _layer_spec(layer, d, tn, lambda i, j: (0, j)),
            pl.BlockSpec((1, tn), lambda i, j: (0, j)),
        ],
        out_specs=pl.BlockSpec((tm, tn), lambda i, j: (i, j)),
        out_shape=jax.ShapeDtypeStruct((s, n), BF16),
        scratch_shapes=[pltpu.VMEM((tm, d), BF16)],
        compiler_params=_params("parallel", "arbitrary"),
        name="norm_proj",
    )(x, g.reshape(1, d), w, jnp.asarray(col_scale, F32).reshape(1, n))


def _col_scale(n, scaled):
    cs = np.ones((n,), np.float32)
    for lo, hi, val in scaled:
        cs[lo:hi] = val
    return cs


def _block_start(kb, t):
    return kb * t if isinstance(kb, int) else pl.multiple_of(kb * t, t)


ONES_ROWS = 16


def _values_transposed(src_ref, dst_ref, seq, chunk):
    def body(c, carry):
        r0 = pl.multiple_of(c * chunk, chunk)
        dst_ref[0:HEAD_DIM, pl.ds(r0, chunk)] = src_ref[pl.ds(r0, chunk), :].astype(F32).T.astype(BF16)
        return carry
    lax.fori_loop(0, seq // chunk, body, 0)
    dst_ref[HEAD_DIM:, :] = jnp.ones((ONES_ROWS, seq), BF16)


def _diff_body(slopes_ref, lam_ref, q_ref, k_ref, v_ref, sg_ref, o_ref,
               vt_ref, r_ref, u0_ref, u1_ref, mx0_ref, mx1_ref, acc_ref, *, t, seq, out_scale):
    h = pl.program_id(0)
    i = pl.program_id(1)
    nk = seq // t
    dk = HEAD_DIM // 2
    slope = slopes_ref[h]
    u_refs = (u0_ref, u1_ref)
    mx_refs = (mx0_ref, mx1_ref)

    @pl.when(i == 0)
    def _():
        _values_transposed(v_ref, vt_ref, seq, t)
        rel = (lax.broadcasted_iota(jnp.int32, (t, t), 0)
               - lax.broadcasted_iota(jnp.int32, (t, t), 1)).astype(F32)
        r = slope * rel
        r_ref[0] = r
        r_ref[1] = -r
        r_ref[2] = -jnp.abs(r)

    q = q_ref[...]
    lane = lax.broadcasted_iota(jnp.int32, q.shape, 1)
    zero = jnp.zeros_like(q)
    q_half = (jnp.where(lane < dk, q, zero), jnp.where(lane >= dk, q, zero))
    acc_ref[...] = jnp.zeros_like(acc_ref)

    def scores(kb, slot):
        side = jnp.where(kb < i, 0, jnp.where(kb > i, 1, 2))
        k = k_ref[pl.ds(_block_start(kb, t), t), :]
        r = r_ref[side]
        for c in range(2):
            u = lax.dot_general(k, q_half[c], _NT, preferred_element_type=F32) + r
            u_refs[slot][c] = u
            mx_refs[slot][c] = jnp.max(u, axis=0, keepdims=True)

    def update(kb, slot, carry):
        vt = vt_ref[:, pl.ds(_block_start(kb, t), t)]
        off = slope * (jnp.abs(kb - i) * t).astype(F32)
        new = []
        for c in range(2):
            m_old = carry[c]
            m_new = jnp.maximum(m_old, mx_refs[slot][c] - off)
            alpha = jnp.exp2(m_old - m_new)
            p = jnp.exp2(u_refs[slot][c] - (m_new + off))
            acc_ref[c] = alpha * acc_ref[c] + jnp.dot(vt, p.astype(BF16), preferred_element_type=F32)
            new.append(m_new)
        return tuple(new)

    def pair(j, carry):
        kb = 2 * j
        scores(kb + 1, 1)
        carry = update(kb, 0, carry)
        scores(kb + 2, 0)
        return update(kb + 1, 1, carry)

    m_init = jnp.full((1, t), NEG_INF, F32)
    scores(0, 0)
    carry = lax.fori_loop(0, nk // 2 - 1, pair, (m_init, m_init))
    scores(nk - 1, 1)
    carry = update(nk - 2, 0, carry)
    update(nk - 1, 1, carry)

    lam = lam_ref[0]
    o0 = acc_ref[0, 0:HEAD_DIM] / acc_ref[0, HEAD_DIM:HEAD_DIM + 1]
    o1 = acc_ref[1, 0:HEAD_DIM] / acc_ref[1, HEAD_DIM:HEAD_DIM + 1]
    ot = o0 - lam * o1
    o = _rms(ot.T, sg_ref[...]) * out_scale
    o_ref[...] = o.astype(o_ref.dtype)


def _diff_attention(qkv, lam, subln, lambda_init, *, n_heads, q_col, k_col, v_col, t=512):
    seq = qkv.shape[0]
    assert (seq // t) % 2 == 0
    slopes = jnp.asarray(_alibi_slopes(n_heads) * np.float32(LOG2E))
    smem = pl.BlockSpec(memory_space=pltpu.SMEM)
    tile = pltpu.VMEM((2, t, t), F32)
    colmax = pltpu.VMEM((2, 1, t), F32)
    return pl.pallas_call(
        functools.partial(_diff_body, t=t, seq=seq, out_scale=1.0 - lambda_init),
        grid=(n_heads, seq // t),
        in_specs=[
            smem,
            smem,
            pl.BlockSpec((t, HEAD_DIM), lambda h, i: (i, q_col + h)),
            pl.BlockSpec((seq, HEAD_DIM), lambda h, i: (0, k_col + h)),
            pl.BlockSpec((seq, HEAD_DIM), lambda h, i: (0, v_col + h)),
            pl.BlockSpec((1, HEAD_DIM), lambda h, i: (0, 0)),
        ],
        out_specs=pl.BlockSpec((t, HEAD_DIM), lambda h, i: (i, h)),
        out_shape=jax.ShapeDtypeStruct((seq, n_heads * HEAD_DIM), BF16),
        scratch_shapes=[pltpu.VMEM((HEAD_DIM + ONES_ROWS, seq), BF16), pltpu.VMEM((3, t, t), F32),
                        tile, tile, colmax, colmax, pltpu.VMEM((2, HEAD_DIM + ONES_ROWS, t), F32)],
        compiler_params=_params("parallel", "arbitrary"),
        name="diff_attn",
    )(slopes, lam.reshape(1).astype(F32), qkv, qkv, qkv, subln.reshape(1, HEAD_DIM))


NA_ROWS_Q = 4
NA_ROWS_K = 12


def _na_bias_table(rpb, rows):
    kh, kw, w = NA_KH, NA_KW, GRID_W
    n_heads = rpb.shape[0]
    nblk = rows // NA_ROWS_Q
    rpb = rpb.astype(F32) * LOG2E
    ext_idx = np.clip(np.arange(2 * w - 1) - (w - 1) + kw - 1, 0, 2 * kw - 2)
    lo = int(np.argmax(ext_idx > 0))
    hi = int(np.argmax(ext_idx == 2 * kw - 2))
    ext = jnp.concatenate([jnp.repeat(rpb[..., :1], lo - 1, axis=-1), rpb,
                           jnp.repeat(rpb[..., -1:], 2 * w - 1 - hi - 1, axis=-1)], axis=-1)
    tab = jnp.stack([ext[..., w - 1 - qc:2 * w - 1 - qc] for qc in range(w)], axis=-2)
    c = np.arange(w)
    c0 = np.clip(c - kw // 2, 0, w - kw)
    col_ok = (c[None, :] >= c0[:, None]) & (c[None, :] < c0[:, None] + kw)
    tab = jnp.where(col_ok, tab, NEG_INF)
    masked = jnp.full((n_heads, w, w), NEG_INF, F32)
    cases = []
    for blk in (0, 1, nblk - 1):
        ws = int(np.clip(NA_ROWS_Q * blk - kh // 2, 0, rows - NA_ROWS_K))
        q_rows = []
        for a in range(NA_ROWS_Q):
            r = NA_ROWS_Q * blk + a
            r0 = int(np.clip(r - kh // 2, 0, rows - kh))
            tiles = []
            for wi in range(NA_ROWS_K):
                kr = ws + wi
                tiles.append(tab[:, kr - r + kh - 1] if r0 <= kr < r0 + kh else masked)
            q_rows.append(jnp.concatenate(tiles, axis=-1))
        cases.append(jnp.concatenate(q_rows, axis=-2))
    return jnp.stack(cases, axis=1).swapaxes(-1, -2)


def _na_body(q_ref, k_ref, v_ref, b_ref, o_ref, vt_ref, u0_ref, u1_ref, *, rows):
    seq = rows * GRID_W
    nblk = rows // NA_ROWS_Q
    tq = NA_ROWS_Q * GRID_W
    nkeys = NA_ROWS_K * GRID_W
    u_refs = (u0_ref, u1_ref)
    _values_transposed(v_ref, vt_ref, seq, 512)

    def window(rb):
        ws = jnp.clip(NA_ROWS_Q * rb - NA_KH // 2, 0, rows - NA_ROWS_K)
        return pl.multiple_of(ws * GRID_W, NA_ROWS_Q * GRID_W)

    def scores(rb, slot):
        case = jnp.where(rb == 0, 0, jnp.where(rb == nblk - 1, 2, 1))
        k = k_ref[pl.ds(window(rb), nkeys), :]
        q = q_ref[pl.ds(_block_start(rb, tq), tq), :]
        u_refs[slot][...] = lax.dot_general(k, q, _NT, preferred_element_type=F32) + b_ref[case]

    def finish(rb, slot):
        u = u_refs[slot][...]
        p = jnp.exp2(u - jnp.max(u, axis=0, keepdims=True))
        vt = vt_ref[:, pl.ds(window(rb), nkeys)]
        acc = jnp.dot(vt, p.astype(BF16), preferred_element_type=F32)
        o = (acc[0:HEAD_DIM] / acc[HEAD_DIM:HEAD_DIM + 1]).T
        o_ref[pl.ds(_block_start(rb, tq), tq), :] = o.astype(o_ref.dtype)

    def pair(j, carry):
        rb = 2 * j
        scores(rb + 1, 1)
        finish(rb, 0)
        scores(rb + 2, 0)
        finish(rb + 1, 1)
        return carry

    scores(0, 0)
    lax.fori_loop(0, nblk // 2 - 1, pair, 0)
    scores(nblk - 1, 1)
    finish(nblk - 2, 0)
    finish(nblk - 1, 1)


def _neighborhood_attention(qkv, rpb, *, n_heads, q_col, k_col, v_col):
    seq = qkv.shape[0]
    rows = seq // GRID_W
    assert (rows // NA_ROWS_Q) % 2 == 0 and NA_ROWS_Q == NA_KH // 2
    tq = NA_ROWS_Q * GRID_W
    nkeys = NA_ROWS_K * GRID_W
    bias = _na_bias_table(rpb, rows)
    head_cols = lambda col: pl.BlockSpec((seq, HEAD_DIM), lambda h: (0, col + h))
    return pl.pallas_call(
        functools.partial(_na_body, rows=rows),
        grid=(n_heads,),
        in_specs=[
            head_cols(q_col),
            head_cols(k_col),
            head_cols(v_col),
            pl.BlockSpec((None, 3, nkeys, tq), lambda h: (h, 0, 0, 0)),
        ],
        out_specs=head_cols(0),
        out_shape=jax.ShapeDtypeStruct((seq, n_heads * HEAD_DIM), BF16),
        scratch_shapes=[pltpu.VMEM((HEAD_DIM + ONES_ROWS, seq), BF16),
                        pltpu.VMEM((nkeys, tq), F32), pltpu.VMEM((nkeys, tq), F32)],
        compiler_params=_params("arbitrary"),
        name="na_attn",
    )(qkv, qkv, qkv, bias)


def _dil_body(slopes_ref, q_ref, k_ref, v_ref, o_ref, vt_ref, b_ref, u0_ref, u1_ref, acc_ref,
              *, t, seq, nb):
    h = pl.program_id(0)
    i = pl.program_id(1)
    nk = seq // t
    nsteps = 2 * nb + 1
    slope = slopes_ref[h]
    u_refs = (u0_ref, u1_ref)

    @pl.when(i == 0)
    def _():
        _values_transposed(v_ref, vt_ref, seq, t)
        rel = (lax.broadcasted_iota(jnp.int32, (t, t), 0)
               - lax.broadcasted_iota(jnp.int32, (t, t), 1))
        for dlt in range(-nb, nb + 1):
            ad = jnp.abs(rel + dlt * t)
            cnt = jnp.zeros((t, t), jnp.int32)
            for window, dil in DILATED_PAIRS:
                ok = (ad <= window // 2) & ((ad & (dil - 1)) == 0)
                cnt = cnt + jnp.where(ok, 1, 0)
            logc = jnp.where(cnt == 3, math.log2(3.0), jnp.where(cnt == 2, 1.0, 0.0))
            b_ref[dlt + nb] = jnp.where(cnt > 0, logc - slope * ad.astype(F32), NEG_INF)
        b_ref[nsteps] = jnp.full((t, t), NEG_INF, F32)

    q = q_ref[...]
    acc_ref[...] = jnp.zeros_like(acc_ref)

    def scores(step, slot):
        kb = i - nb + step
        inside = (kb >= 0) & (kb < nk)
        k = k_ref[pl.ds(_block_start(jnp.clip(kb, 0, nk - 1), t), t), :]
        u = lax.dot_general(k, q, _NT, preferred_element_type=F32) + b_ref[jnp.where(inside, step, nsteps)]
        u_refs[slot][...] = u
        return jnp.max(u, axis=0, keepdims=True)

    def update(step, slot, mx, m_old):
        kb = jnp.clip(i - nb + step, 0, nk - 1)
        vt = vt_ref[:, pl.ds(_block_start(kb, t), t)]
        m_new = jnp.maximum(m_old, mx)
        alpha = jnp.exp2(m_old - m_new)
        p = jnp.exp2(u_refs[slot][...] - m_new)
        acc_ref[...] = alpha * acc_ref[...] + jnp.dot(vt, p.astype(BF16), preferred_element_type=F32)
        return m_new

    m = jnp.full((1, t), NEG_INF, F32)
    mx = scores(0, 0)
    for step in range(nsteps):
        mx_next = scores(step + 1, (step + 1) % 2) if step + 1 < nsteps else None
        m = update(step, step % 2, mx, m)
        mx = mx_next
    o_ref[...] = (acc_ref[0:HEAD_DIM] / acc_ref[HEAD_DIM:HEAD_DIM + 1]).T.astype(o_ref.dtype)


def _dilated_attention(qkv, *, n_heads, t=512):
    seq = qkv.shape[0]
    for window, dil in DILATED_PAIRS:
        assert dil & (dil - 1) == 0 and window % (2 * dil) == 0
    reach = max(window // 2 for window, _ in DILATED_PAIRS)
    nb = (reach - 1) // t + 1
    slopes = jnp.asarray(_alibi_slopes(n_heads) * np.float32(LOG2E))
    return pl.pallas_call(
        functools.partial(_dil_body, t=t, seq=seq, nb=nb),
        grid=(n_heads, seq // t),
        in_specs=[
            pl.BlockSpec(memory_space=pltpu.SMEM),
            pl.BlockSpec((t, HEAD_DIM), lambda h, i: (i, h)),
            pl.BlockSpec((seq, HEAD_DIM), lambda h, i: (0, n_heads + h)),
            pl.BlockSpec((seq, HEAD_DIM), lambda h, i: (0, 2 * n_heads + h)),
        ],
        out_specs=pl.BlockSpec((t, HEAD_DIM), lambda h, i: (i, h)),
        out_shape=jax.ShapeDtypeStruct((seq, n_heads * HEAD_DIM), BF16),
        scratch_shapes=[pltpu.VMEM((HEAD_DIM + ONES_ROWS, seq), BF16), pltpu.VMEM((2 * nb + 2, t, t), F32),
                        pltpu.VMEM((t, t), F32), pltpu.VMEM((t, t), F32),
                        pltpu.VMEM((HEAD_DIM + ONES_ROWS, t), F32)],
        compiler_params=_params("parallel", "arbitrary"),
        name="dilated_attn",
    )(slopes, qkv, qkv, qkv)


def _post_mix_body(a_ref, b_ref, wa_ref, wb_ref, x_ref, g_ref, wq_ref, kv_ref, wo_ref, o_ref, *, scale):
    y = x_ref[...] + jnp.dot(a_ref[...], wa_ref[...], preferred_element_type=F32)
    y = y + jnp.dot(b_ref[...], wb_ref[...], preferred_element_type=F32)
    h = _rms(y, g_ref[...]).astype(BF16)
    q = jnp.dot(h, wq_ref[...], preferred_element_type=F32).astype(BF16)
    width = N_HEADS_MEM * HEAD_DIM
    outs = []
    for hd in range(N_HEADS_MEM):
        lo, hi = hd * HEAD_DIM, (hd + 1) * HEAD_DIM
        s = lax.dot_general(q[:, lo:hi], kv_ref[:, lo:hi], _NT, preferred_element_type=F32) * scale
        m = jnp.max(s, axis=-1, keepdims=True)
        p = jnp.exp(s - m)
        l = jnp.sum(p, axis=-1, keepdims=True)
        o = jnp.dot(p.astype(BF16), kv_ref[:, width + lo:width + hi], preferred_element_type=F32) / l
        outs.append(o.astype(BF16))
    o = jnp.concatenate(outs, axis=-1)
    o_ref[...] = y + jnp.dot(o, wo_ref[...], preferred_element_type=F32)


def _post_mix(a, a_col, b, b_col, w_out, x, g, wq, kv, wo, layer, *, tm=256):
    s, d = x.shape
    kh = w_out.shape[1] // 2
    n_mem, kvw = kv.shape
    width = wq.shape[2]
    once = pl.Buffered(1)

    def resident(rows, cols, r):
        return pl.BlockSpec((None, rows, cols), lambda i: (layer, r, 0), pipeline_mode=once)

    return pl.pallas_call(
        functools.partial(_post_mix_body, scale=HEAD_DIM ** -0.5),
        grid=(s // tm,),
        in_specs=[
            pl.BlockSpec((tm, kh), lambda i: (i, a_col)),
            pl.BlockSpec((tm, kh), lambda i: (i, b_col)),
            resident(kh, d, 0),
            resident(kh, d, 1),
            pl.BlockSpec((tm, d), lambda i: (i, 0)),
            pl.BlockSpec((1, d), lambda i: (0, 0)),
            resident(d, width, 0),
            pl.BlockSpec((n_mem, kvw), lambda i: (0, 0), pipeline_mode=once),
            resident(width, d, 0),
        ],
        out_specs=pl.BlockSpec((tm, d), lambda i: (i, 0)),
        out_shape=jax.ShapeDtypeStruct((s, d), F32),
        compiler_params=_params("parallel"),
        name="post_mix",
    )(a, b, w_out, w_out, x, g.reshape(1, d), wq, kv, wo)


def kernel(x, mem, ffn1_norm, ffn1_w_gate, ffn1_w_up, ffn1_w_down, mix_norm, mix_w_in, mix_w_out, diff_lq1, diff_lk1, diff_lq2, diff_lk2, diff_subln, na_rpb, mem_q_norm, mem_kv_norm, mem_wq, mem_wkv, mem_wo, ffn2_norm, ffn2_w_gate, ffn2_w_up, ffn2_w_down, final_norm):
    batch, seq, d_model = x.shape
    assert batch == 1
    depth = ffn1_norm.shape[0]
    n_heads = d_model // HEAD_DIM
    in_proj = mix_w_in.shape[2]
    xs = x.reshape(seq, d_model)
    mem2 = mem.reshape(mem.shape[1], d_model)

    bf = lambda t: t.astype(BF16)
    w1g, w1u, w1d = ffn1_w_gate, ffn1_w_up, ffn1_w_down
    w2g, w2u, w2d = ffn2_w_gate, ffn2_w_up, ffn2_w_down
    w_in, w_out = bf(mix_w_in), bf(mix_w_out)
    wq, wkv, wo = bf(mem_wq), bf(mem_wkv), bf(mem_wo)

    hh = n_heads // 2
    half_w = hh * HEAD_DIM
    cs_even = _col_scale(in_proj, [(0, half_w, (HEAD_DIM // 2) ** -0.5 * LOG2E),
                                   (3 * half_w, 4 * half_w, HEAD_DIM ** -0.5 * LOG2E)])
    cs_odd = _col_scale(in_proj, [(0, d_model, HEAD_DIM ** -0.5 * LOG2E)])
    cs_kv = _col_scale(mem_wkv.shape[2], [])

    for i in range(depth):
        xs = _ffn(xs, ffn1_norm[i], w1g, w1u, w1d, i)
        if i % 2 == 0:
            e = i // 2
            qkv = _norm_proj(xs, mix_norm[i], w_in, i, cs_even)
            lambda_init = 0.8 - 0.6 * math.exp(-0.3 * i)
            lam = (jnp.exp(jnp.sum(diff_lq1[e].astype(F32) * diff_lk1[e].astype(F32)))
                   - jnp.exp(jnp.sum(diff_lq2[e].astype(F32) * diff_lk2[e].astype(F32)))
                   + lambda_init)
            oa = _diff_attention(qkv, lam, diff_subln[e], lambda_init, n_heads=hh,
                                 q_col=0, k_col=hh, v_col=2 * hh)
            ob = _neighborhood_attention(qkv, na_rpb[e], n_heads=hh,
                                         q_col=3 * hh, k_col=4 * hh, v_col=5 * hh)
            mix = (oa, 0, ob, 0)
        else:
            qkv = _norm_proj(xs, mix_norm[i], w_in, i, cs_odd)
            o = _dilated_attention(qkv, n_heads=n_heads)
            mix = (o, 0, o, 1)
        kv = _norm_proj(mem2, mem_kv_norm[i], wkv, i, cs_kv)
        xs = _post_mix(*mix, w_out, xs, mem_q_norm[i], wq, kv, wo, i)
        last = i == depth - 1
        xs = _ffn(xs, ffn2_norm[i], w2g, w2u, w2d, i, final_g=final_norm if last else None)
    return xs.reshape(batch, seq, d_model)
```

```python
import functools
import math

import numpy as np
import jax
import jax.numpy as jnp
from jax import lax
from jax.experimental import pallas as pl
from jax.experimental.pallas import tpu as pltpu

F32 = jnp.float32
BF16 = jnp.bfloat16

EPS = 1e-6
NEG_INF = -1e30
LOG2E = math.log2(math.e)
HEAD_DIM = 128
GRID_W = 64
NA_KH = 8
NA_KW = 16
N_HEADS_MEM = 4
DILATED_PAIRS = ((128, 1), (512, 4), (2048, 16))

VMEM_LIMIT = 52 * 1024 * 1024

_NT = (((1,), (1,)), ((), ()))


def _params(*sem):
    return pltpu.CompilerParams(dimension_semantics=sem, vmem_limit_bytes=VMEM_LIMIT)


def _rms(x, g):
    return x * lax.rsqrt(jnp.mean(x * x, axis=-1, keepdims=True) + EPS) * g


def _alibi_slopes(n):
    return np.exp2(-8.0 * np.arange(1, n + 1) / n).astype(np.float32)


def _layer_spec(layer, rows, cols, index_map):
    return pl.BlockSpec((None, rows, cols), lambda *g: (layer,) + tuple(index_map(*g)))


def _ffn_body(x_ref, g_ref, wg_ref, wu_ref, wd_ref, fg_ref, o_ref, h_ref, *, final_norm):
    j = pl.program_id(1)

    @pl.when(j == 0)
    def _():
        x = x_ref[...]
        h_ref[...] = _rms(x, g_ref[...]).astype(BF16)
        o_ref[...] = x

    h = h_ref[...]
    a = jnp.dot(h, wg_ref[...].astype(BF16), preferred_element_type=F32)
    b = jnp.dot(h, wu_ref[...].astype(BF16), preferred_element_type=F32)
    act = (0.5 * a * jax.nn.sigmoid(a)) * b
    o_ref[...] += jnp.dot(act.astype(BF16), wd_ref[...].astype(BF16), preferred_element_type=F32)

    if final_norm:
        @pl.when(j == pl.num_programs(1) - 1)
        def _():
            o_ref[...] = _rms(o_ref[...], fg_ref[...])


FFN_TF = 256
PROJ_TN = 1024


def _ffn(x, g, wg, wu, wd, layer, final_g=None, *, tm=1024):
    s, d = x.shape
    tf = FFN_TF
    f = wd.shape[1]
    final_norm = final_g is not None
    fg = final_g if final_norm else g
    return pl.pallas_call(
        functools.partial(_ffn_body, final_norm=final_norm),
        grid=(s // tm, f // tf),
        in_specs=[
            pl.BlockSpec((tm, d), lambda i, j: (i, 0)),
            pl.BlockSpec((1, d), lambda i, j: (0, 0)),
            _layer_spec(layer, d, tf, lambda i, j: (0, j)),
            _layer_spec(layer, d, tf, lambda i, j: (0, j)),
            _layer_spec(layer, tf, d, lambda i, j: (j, 0)),
            pl.BlockSpec((1, d), lambda i, j: (0, 0)),
        ],
        out_specs=pl.BlockSpec((tm, d), lambda i, j: (i, 0)),
        out_shape=jax.ShapeDtypeStruct((s, d), F32),
        scratch_shapes=[pltpu.VMEM((tm, d), BF16)],
        compiler_params=_params("parallel", "arbitrary"),
        name="ffn",
    )(x, g.reshape(1, d), wg, wu, wd, fg.reshape(1, d))


def _norm_proj_body(x_ref, g_ref, w_ref, cs_ref, o_ref, h_ref):
    @pl.when(pl.program_id(1) == 0)
    def _():
        h_ref[...] = _rms(x_ref[...], g_ref[...]).astype(BF16)

    acc = jnp.dot(h_ref[...], w_ref[...], preferred_element_type=F32)
    o_ref[...] = (acc * cs_ref[...]).astype(o_ref.dtype)


def _norm_proj(x, g, w, layer, col_scale, *, tm=1024):
    s, d = x.shape
    tn = PROJ_TN
    n = w.shape[2]
    tm = min(tm, s)
    return pl.pallas_call(
        _norm_proj_body,
        grid=(s // tm, n // tn),
        in_specs=[
            pl.BlockSpec((tm, d), lambda i, j: (i, 0)),
            pl.BlockSpec((1, d), lambda i, j: (0, 0)),
            _layer_spec(layer, d, tn, lambda i, j: (0, j)),
            pl.BlockSpec((1, tn), lambda i, j: (0, j)),
        ],
        out_specs=pl.BlockSpec((tm, tn), lambda i, j: (i, j)),
        out_shape=jax.ShapeDtypeStruct((s, n), BF16),
        scratch_shapes=[pltpu.VMEM((tm, d), BF16)],
        compiler_params=_params("parallel", "arbitrary"),
        name="norm_proj",
    )(x, g.reshape(1, d), w, jnp.asarray(col_scale, F32).reshape(1, n))


def _col_scale(n, scaled):
    cs = np.ones((n,), np.float32)
    for lo, hi, val in scaled:
        cs[lo:hi] = val
    return cs


def _block_start(kb, t):
    return kb * t if isinstance(kb, int) else pl.multiple_of(kb * t, t)


ONES_ROWS = 16
UNROLL = 4


def _values_transposed(src_ref, dst_ref, seq, chunk):
    def body(c, carry):
        r0 = pl.multiple_of(c * chunk, chunk)
        dst_ref[0:HEAD_DIM, pl.ds(r0, chunk)] = src_ref[pl.ds(r0, chunk), :].astype(F32).T.astype(BF16)
        return carry
    lax.fori_loop(0, seq // chunk, body, 0)
    dst_ref[HEAD_DIM:, :] = jnp.ones((ONES_ROWS, seq), BF16)


def _diff_body(slopes_ref, lam_ref, q_ref, k_ref, v_ref, sg_ref, o_ref,
               vt_ref, r_ref, u0_ref, u1_ref, mx0_ref, mx1_ref, acc_ref, *, t, seq, out_scale):
    h = pl.program_id(0)
    i = pl.program_id(1)
    nk = seq // t
    dk = HEAD_DIM // 2
    slope = slopes_ref[h]
    u_refs = (u0_ref, u1_ref)
    mx_refs = (mx0_ref, mx1_ref)

    @pl.when(i == 0)
    def _():
        _values_transposed(v_ref, vt_ref, seq, t)
        rel = (lax.broadcasted_iota(jnp.int32, (t, t), 0)
               - lax.broadcasted_iota(jnp.int32, (t, t), 1)).astype(F32)
        r = slope * rel
        r_ref[0] = r
        r_ref[1] = -r
        r_ref[2] = -jnp.abs(r)

    q = q_ref[...]
    lane = lax.broadcasted_iota(jnp.int32, q.shape, 1)
    zero = jnp.zeros_like(q)
    q_half = (jnp.where(lane < dk, q, zero), jnp.where(lane >= dk, q, zero))
    acc_ref[...] = jnp.zeros_like(acc_ref)

    def scores(kb, slot):
        side = jnp.where(kb < i, 0, jnp.where(kb > i, 1, 2))
        k = k_ref[pl.ds(_block_start(kb, t), t), :]
        r = r_ref[side]
        for c in range(2):
            u = lax.dot_general(k, q_half[c], _NT, preferred_element_type=F32) + r
            u_refs[slot][c] = u
            mx_refs[slot][c] = jnp.max(u, axis=0, keepdims=True)

    def update(kb, slot, carry):
        vt = vt_ref[:, pl.ds(_block_start(kb, t), t)]
        off = slope * (jnp.abs(kb - i) * t).astype(F32)
        new = []
        for c in range(2):
            m_old = carry[c]
            m_new = jnp.maximum(m_old, mx_refs[slot][c] - off)
            alpha = jnp.exp2(m_old - m_new)
            p = jnp.exp2(u_refs[slot][c] - (m_new + off))
            acc_ref[c] = alpha * acc_ref[c] + jnp.dot(vt, p.astype(BF16), preferred_element_type=F32)
            new.append(m_new)
        return tuple(new)

    def group(j, carry):
        for s in range(UNROLL):
            kb = UNROLL * j + s
            scores(kb + 1, (s + 1) % 2)
            carry = update(kb, s % 2, carry)
        return carry

    m_init = jnp.full((1, t), NEG_INF, F32)
    scores(0, 0)
    carry = lax.fori_loop(0, nk // UNROLL - 1, group, (m_init, m_init))
    for kb in range(nk - UNROLL, nk):
        if kb + 1 < nk:
            scores(kb + 1, (kb + 1) % 2)
        carry = update(kb, kb % 2, carry)

    lam = lam_ref[0]
    o0 = acc_ref[0, 0:HEAD_DIM] / acc_ref[0, HEAD_DIM:HEAD_DIM + 1]
    o1 = acc_ref[1, 0:HEAD_DIM] / acc_ref[1, HEAD_DIM:HEAD_DIM + 1]
    ot = o0 - lam * o1
    o = _rms(ot.T, sg_ref[...]) * out_scale
    o_ref[...] = o.astype(o_ref.dtype)


def _diff_attention(qkv, lam, subln, lambda_init, *, n_heads, q_col, k_col, v_col, t=512):
    seq = qkv.shape[0]
    assert UNROLL % 2 == 0 and (seq // t) % UNROLL == 0
    slopes = jnp.asarray(_alibi_slopes(n_heads) * np.float32(LOG2E))
    smem = pl.BlockSpec(memory_space=pltpu.SMEM)
    tile = pltpu.VMEM((2, t, t), F32)
    colmax = pltpu.VMEM((2, 1, t), F32)
    return pl.pallas_call(
        functools.partial(_diff_body, t=t, seq=seq, out_scale=1.0 - lambda_init),
        grid=(n_heads, seq // t),
        in_specs=[
            smem,
            smem,
            pl.BlockSpec((t, HEAD_DIM), lambda h, i: (i, q_col + h)),
            pl.BlockSpec((seq, HEAD_DIM), lambda h, i: (0, k_col + h)),
            pl.BlockSpec((seq, HEAD_DIM), lambda h, i: (0, v_col + h)),
            pl.BlockSpec((1, HEAD_DIM), lambda h, i: (0, 0)),
        ],
        out_specs=pl.BlockSpec((t, HEAD_DIM), lambda h, i: (i, h)),
        out_shape=jax.ShapeDtypeStruct((seq, n_heads * HEAD_DIM), BF16),
        scratch_shapes=[pltpu.VMEM((HEAD_DIM + ONES_ROWS, seq), BF16), pltpu.VMEM((3, t, t), F32),
                        tile, tile, colmax, colmax, pltpu.VMEM((2, HEAD_DIM + ONES_ROWS, t), F32)],
        compiler_params=_params("parallel", "arbitrary"),
        name="diff_attn",
    )(slopes, lam.reshape(1).astype(F32), qkv, qkv, qkv, subln.reshape(1, HEAD_DIM))


NA_ROWS_Q = 4
NA_ROWS_K = 12


def _na_bias_table(rpb, rows):
    kh, kw, w = NA_KH, NA_KW, GRID_W
    n_heads = rpb.shape[0]
    nblk = rows // NA_ROWS_Q
    rpb = rpb.astype(F32) * LOG2E
    ext_idx = np.clip(np.arange(2 * w - 1) - (w - 1) + kw - 1, 0, 2 * kw - 2)
    lo = int(np.argmax(ext_idx > 0))
    hi = int(np.argmax(ext_idx == 2 * kw - 2))
    ext = jnp.concatenate([jnp.repeat(rpb[..., :1], lo - 1, axis=-1), rpb,
                           jnp.repeat(rpb[..., -1:], 2 * w - 1 - hi - 1, axis=-1)], axis=-1)
    tab = jnp.stack([ext[..., w - 1 - qc:2 * w - 1 - qc] for qc in range(w)], axis=-2)
    c = np.arange(w)
    c0 = np.clip(c - kw // 2, 0, w - kw)
    col_ok = (c[None, :] >= c0[:, None]) & (c[None, :] < c0[:, None] + kw)
    tab = jnp.where(col_ok, tab, NEG_INF)
    masked = jnp.full((n_heads, w, w), NEG_INF, F32)
    cases = []
    for blk in (0, 1, nblk - 1):
        ws = int(np.clip(NA_ROWS_Q * blk - kh // 2, 0, rows - NA_ROWS_K))
        q_rows = []
        for a in range(NA_ROWS_Q):
            r = NA_ROWS_Q * blk + a
            r0 = int(np.clip(r - kh // 2, 0, rows - kh))
            tiles = []
            for wi in range(NA_ROWS_K):
                kr = ws + wi
                tiles.append(tab[:, kr - r + kh - 1] if r0 <= kr < r0 + kh else masked)
            q_rows.append(jnp.concatenate(tiles, axis=-1))
        cases.append(jnp.concatenate(q_rows, axis=-2))
    return jnp.stack(cases, axis=1).swapaxes(-1, -2)


def _na_body(q_ref, k_ref, v_ref, b_ref, o_ref, vt_ref, u0_ref, u1_ref, *, rows):
    seq = rows * GRID_W
    nblk = rows // NA_ROWS_Q
    tq = NA_ROWS_Q * GRID_W
    nkeys = NA_ROWS_K * GRID_W
    u_refs = (u0_ref, u1_ref)
    _values_transposed(v_ref, vt_ref, seq, 512)

    def window(rb):
        ws = jnp.clip(NA_ROWS_Q * rb - NA_KH // 2, 0, rows - NA_ROWS_K)
        return pl.multiple_of(ws * GRID_W, NA_ROWS_Q * GRID_W)

    def scores(rb, slot):
        case = jnp.where(rb == 0, 0, jnp.where(rb == nblk - 1, 2, 1))
        k = k_ref[pl.ds(window(rb), nkeys), :]
        q = q_ref[pl.ds(_block_start(rb, tq), tq), :]
        u_refs[slot][...] = lax.dot_general(k, q, _NT, preferred_element_type=F32) + b_ref[case]

    def finish(rb, slot):
        u = u_refs[slot][...]
        p = jnp.exp2(u - jnp.max(u, axis=0, keepdims=True))
        vt = vt_ref[:, pl.ds(window(rb), nkeys)]
        acc = jnp.dot(vt, p.astype(BF16), preferred_element_type=F32)
        o = (acc[0:HEAD_DIM] / acc[HEAD_DIM:HEAD_DIM + 1]).T
        o_ref[pl.ds(_block_start(rb, tq), tq), :] = o.astype(o_ref.dtype)

    def pair(j, carry):
        rb = 2 * j
        scores(rb + 1, 1)
        finish(rb, 0)
        scores(rb + 2, 0)
        finish(rb + 1, 1)
        return carry

    scores(0, 0)
    lax.fori_loop(0, nblk // 2 - 1, pair, 0)
    scores(nblk - 1, 1)
    finish(nblk - 2, 0)
    finish(nblk - 1, 1)


def _neighborhood_attention(qkv, rpb, *, n_heads, q_col, k_col, v_col):
    seq = qkv.shape[0]
    rows = seq // GRID_W
    assert (rows // NA_ROWS_Q) % 2 == 0 and NA_ROWS_Q == NA_KH // 2
    tq = NA_ROWS_Q * GRID_W
    nkeys = NA_ROWS_K * GRID_W
    bias = _na_bias_table(rpb, rows)
    head_cols = lambda col: pl.BlockSpec((seq, HEAD_DIM), lambda h: (0, col + h))
    return pl.pallas_call(
        functools.partial(_na_body, rows=rows),
        grid=(n_heads,),
        in_specs=[
            head_cols(q_col),
            head_cols(k_col),
            head_cols(v_col),
            pl.BlockSpec((None, 3, nkeys, tq), lambda h: (h, 0, 0, 0)),
        ],
        out_specs=head_cols(0),
        out_shape=jax.ShapeDtypeStruct((seq, n_heads * HEAD_DIM), BF16),
        scratch_shapes=[pltpu.VMEM((HEAD_DIM + ONES_ROWS, seq), BF16),
                        pltpu.VMEM((nkeys, tq), F32), pltpu.VMEM((nkeys, tq), F32)],
        compiler_params=_params("arbitrary"),
        name="na_attn",
    )(qkv, qkv, qkv, bias)


def _dilated_band_tables(t, nb):
    rel = np.arange(t)[:, None] - np.arange(t)[None, :]
    dist, logc = [], []
    for dlt in range(-nb, nb + 1):
        ad = np.abs(rel + dlt * t)
        cnt = sum(((ad <= window // 2) & (ad % dil == 0)).astype(np.int64) for window, dil in DILATED_PAIRS)
        dist.append(np.where(cnt > 0, ad, 0))
        logc.append(np.where(cnt > 0, np.log2(np.maximum(cnt, 1)), NEG_INF))
    return np.stack(dist).astype(np.float32), np.stack(logc).astype(np.float32)


def _dil_body(slopes_ref, q_ref, k_ref, v_ref, dist_ref, logc_ref, o_ref, vt_ref, b_ref, u0_ref, u1_ref,
              acc_ref, *, t, seq, nb):
    h = pl.program_id(0)
    i = pl.program_id(1)
    nk = seq // t
    nsteps = 2 * nb + 1
    slope = slopes_ref[h]
    u_refs = (u0_ref, u1_ref)

    @pl.when(i == 0)
    def _():
        _values_transposed(v_ref, vt_ref, seq, t)
        for s in range(nsteps):
            b_ref[s] = logc_ref[s] - slope * dist_ref[s]
        b_ref[nsteps] = jnp.full((t, t), NEG_INF, F32)

    q = q_ref[...]
    acc_ref[...] = jnp.zeros_like(acc_ref)

    def scores(step, slot):
        kb = i - nb + step
        inside = (kb >= 0) & (kb < nk)
        k = k_ref[pl.ds(_block_start(jnp.clip(kb, 0, nk - 1), t), t), :]
        u = lax.dot_general(k, q, _NT, preferred_element_type=F32) + b_ref[jnp.where(inside, step, nsteps)]
        u_refs[slot][...] = u
        return jnp.max(u, axis=0, keepdims=True)

    def update(step, slot, mx, m_old):
        kb = jnp.clip(i - nb + step, 0, nk - 1)
        vt = vt_ref[:, pl.ds(_block_start(kb, t), t)]
        m_new = jnp.maximum(m_old, mx)
        alpha = jnp.exp2(m_old - m_new)
        p = jnp.exp2(u_refs[slot][...] - m_new)
        acc_ref[...] = alpha * acc_ref[...] + jnp.dot(vt, p.astype(BF16), preferred_element_type=F32)
        return m_new

    m = jnp.full((1, t), NEG_INF, F32)
    mx = scores(0, 0)
    for step in range(nsteps):
        mx_next = scores(step + 1, (step + 1) % 2) if step + 1 < nsteps else None
        m = update(step, step % 2, mx, m)
        mx = mx_next
    o_ref[...] = (acc_ref[0:HEAD_DIM] / acc_ref[HEAD_DIM:HEAD_DIM + 1]).T.astype(o_ref.dtype)


def _dilated_attention(qkv, *, n_heads, t=512):
    seq = qkv.shape[0]
    for window, dil in DILATED_PAIRS:
        assert dil & (dil - 1) == 0 and window % (2 * dil) == 0
    reach = max(window // 2 for window, _ in DILATED_PAIRS)
    nb = (reach - 1) // t + 1
    slopes = jnp.asarray(_alibi_slopes(n_heads) * np.float32(LOG2E))
    dist, logc = _dilated_band_tables(t, nb)
    table = pl.BlockSpec((2 * nb + 1, t, t), lambda h, i: (0, 0, 0), pipeline_mode=pl.Buffered(1))
    return pl.pallas_call(
        functools.partial(_dil_body, t=t, seq=seq, nb=nb),
        grid=(n_heads, seq // t),
        in_specs=[
            pl.BlockSpec(memory_space=pltpu.SMEM),
            pl.BlockSpec((t, HEAD_DIM), lambda h, i: (i, h)),
            pl.BlockSpec((seq, HEAD_DIM), lambda h, i: (0, n_heads + h)),
            pl.BlockSpec((seq, HEAD_DIM), lambda h, i: (0, 2 * n_heads + h)),
            table,
            table,
        ],
        out_specs=pl.BlockSpec((t, HEAD_DIM), lambda h, i: (i, h)),
        out_shape=jax.ShapeDtypeStruct((seq, n_heads * HEAD_DIM), BF16),
        scratch_shapes=[pltpu.VMEM((HEAD_DIM + ONES_ROWS, seq), BF16), pltpu.VMEM((2 * nb + 2, t, t), F32),
                        pltpu.VMEM((t, t), F32), pltpu.VMEM((t, t), F32),
                        pltpu.VMEM((HEAD_DIM + ONES_ROWS, t), F32)],
        compiler_params=_params("parallel", "arbitrary"),
        name="dilated_attn",
    )(slopes, qkv, qkv, qkv, jnp.asarray(dist), jnp.asarray(logc))


def _post_mix_body(a_ref, b_ref, wa_ref, wb_ref, x_ref, g_ref, wq_ref, kv_ref, wo_ref, o_ref, *, scale):
    y = x_ref[...] + jnp.dot(a_ref[...], wa_ref[...], preferred_element_type=F32)
    y = y + jnp.dot(b_ref[...], wb_ref[...], preferred_element_type=F32)
    h = _rms(y, g_ref[...]).astype(BF16)
    q = jnp.dot(h, wq_ref[...], preferred_element_type=F32).astype(BF16)
    width = N_HEADS_MEM * HEAD_DIM
    outs = []
    for hd in range(N_HEADS_MEM):
        lo, hi = hd * HEAD_DIM, (hd + 1) * HEAD_DIM
        s = lax.dot_general(q[:, lo:hi], kv_ref[:, lo:hi], _NT, preferred_element_type=F32) * scale
        m = jnp.max(s, axis=-1, keepdims=True)
        p = jnp.exp(s - m)
        l = jnp.sum(p, axis=-1, keepdims=True)
        o = jnp.dot(p.astype(BF16), kv_ref[:, width + lo:width + hi], preferred_element_type=F32) / l
        outs.append(o.astype(BF16))
    o = jnp.concatenate(outs, axis=-1)
    o_ref[...] = y + jnp.dot(o, wo_ref[...], preferred_element_type=F32)


def _post_mix(a, a_col, b, b_col, w_out, x, g, wq, kv, wo, layer, *, tm=256):
    s, d = x.shape
    kh = w_out.shape[1] // 2
    n_mem, kvw = kv.shape
    width = wq.shape[2]
    once = pl.Buffered(1)

    def resident(rows, cols, r):
        return pl.BlockSpec((None, rows, cols), lambda i: (layer, r, 0), pipeline_mode=once)

    return pl.pallas_call(
        functools.partial(_post_mix_body, scale=HEAD_DIM ** -0.5),
        grid=(s // tm,),
        in_specs=[
            pl.BlockSpec((tm, kh), lambda i: (i, a_col)),
            pl.BlockSpec((tm, kh), lambda i: (i, b_col)),
            resident(kh, d, 0),
            resident(kh, d, 1),
            pl.BlockSpec((tm, d), lambda i: (i, 0)),
            pl.BlockSpec((1, d), lambda i: (0, 0)),
            resident(d, width, 0),
            pl.BlockSpec((n_mem, kvw), lambda i: (0, 0), pipeline_mode=once),
            resident(width, d, 0),
        ],
        out_specs=pl.BlockSpec((tm, d), lambda i: (i, 0)),
        out_shape=jax.ShapeDtypeStruct((s, d), F32),
        compiler_params=_params("parallel"),
        name="post_mix",
    )(a, b, w_out, w_out, x, g.reshape(1, d), wq, kv, wo)


def kernel(x, mem, ffn1_norm, ffn1_w_gate, ffn1_w_up, ffn1_w_down, mix_norm, mix_w_in, mix_w_out, diff_lq1, diff_lk1, diff_lq2, diff_lk2, diff_subln, na_rpb, mem_q_norm, mem_kv_norm, mem_wq, mem_wkv, mem_wo, ffn2_norm, ffn2_w_gate, ffn2_w_up, ffn2_w_down, final_norm):
    batch, seq, d_model = x.shape
    assert batch == 1
    depth = ffn1_norm.shape[0]
    n_heads = d_model // HEAD_DIM
    in_proj = mix_w_in.shape[2]
    xs = x.reshape(seq, d_model)
    mem2 = mem.reshape(mem.shape[1], d_model)

    bf = lambda t: t.astype(BF16)
    w1g, w1u, w1d = ffn1_w_gate, ffn1_w_up, ffn1_w_down
    w2g, w2u, w2d = ffn2_w_gate, ffn2_w_up, ffn2_w_down
    w_in, w_out = bf(mix_w_in), bf(mix_w_out)
    wq, wkv, wo = bf(mem_wq), bf(mem_wkv), bf(mem_wo)

    hh = n_heads // 2
    half_w = hh * HEAD_DIM
    cs_even = _col_scale(in_proj, [(0, half_w, (HEAD_DIM // 2) ** -0.5 * LOG2E),
                                   (3 * half_w, 4 * half_w, HEAD_DIM ** -0.5 * LOG2E)])
    cs_odd = _col_scale(in_proj, [(0, d_model, HEAD_DIM ** -0.5 * LOG2E)])
    cs_kv = _col_scale(mem_wkv.shape[2], [])

    for i in range(depth):
        xs = _ffn(xs, ffn1_norm[i], w1g, w1u, w1d, i)
        if i % 2 == 0:
            e = i // 2
            qkv = _norm_proj(xs, mix_norm[i], w_in, i, cs_even)
            lambda_init = 0.8 - 0.6 * math.exp(-0.3 * i)
            lam = (jnp.exp(jnp.sum(diff_lq1[e].astype(F32) * diff_lk1[e].astype(F32)))
                   - jnp.exp(jnp.sum(diff_lq2[e].astype(F32) * diff_lk2[e].astype(F32)))
                   + lambda_init)
            oa = _diff_attention(qkv, lam, diff_subln[e], lambda_init, n_heads=hh,
                                 q_col=0, k_col=hh, v_col=2 * hh)
            ob = _neighborhood_attention(qkv, na_rpb[e], n_heads=hh,
                                         q_col=3 * hh, k_col=4 * hh, v_col=5 * hh)
            mix = (oa, 0, ob, 0)
        else:
            qkv = _norm_proj(xs, mix_norm[i], w_in, i, cs_odd)
            o = _dilated_attention(qkv, n_heads=n_heads)
            mix = (o, 0, o, 1)
        kv = _norm_proj(mem2, mem_kv_norm[i], wkv, i, cs_kv)
        xs = _post_mix(*mix, w_out, xs, mem_q_norm[i], wq, kv, wo, i)
        last = i == depth - 1
        xs = _ffn(xs, ffn2_norm[i], w2g, w2u, w2d, i, final_g=final_norm if last else None)
    return xs.reshape(batch, seq, d_model)
```

```python
import functools
import math

import numpy as np
import jax
import jax.numpy as jnp
from jax import lax
from jax.experimental import pallas as pl
from jax.experimental.pallas import tpu as pltpu

F32 = jnp.float32
BF16 = jnp.bfloat16

EPS = 1e-6
NEG_INF = -1e30
LOG2E = math.log2(math.e)
HEAD_DIM = 128
GRID_W = 64
NA_KH = 8
NA_KW = 16
N_HEADS_MEM = 4
DILATED_PAIRS = ((128, 1), (512, 4), (2048, 16))

VMEM_LIMIT = 52 * 1024 * 1024

_NT = (((1,), (1,)), ((), ()))


def _params(*sem):
    return pltpu.CompilerParams(dimension_semantics=sem, vmem_limit_bytes=VMEM_LIMIT)


def _rms(x, g):
    return x * lax.rsqrt(jnp.mean(x * x, axis=-1, keepdims=True) + EPS) * g


def _alibi_slopes(n):
    return np.exp2(-8.0 * np.arange(1, n + 1) / n).astype(np.float32)


def _layer_spec(layer, rows, cols, index_map):
    return pl.BlockSpec((None, rows, cols), lambda *g: (layer,) + tuple(index_map(*g)))


def _ffn_body(x_ref, g_ref, wg_ref, wu_ref, wd_ref, fg_ref, o_ref, h_ref, *, final_norm):
    j = pl.program_id(1)

    @pl.when(j == 0)
    def _():
        x = x_ref[...]
        h_ref[...] = _rms(x, g_ref[...]).astype(BF16)
        o_ref[...] = x

    h = h_ref[...]
    a = jnp.dot(h, wg_ref[...].astype(BF16), preferred_element_type=F32)
    b = jnp.dot(h, wu_ref[...].astype(BF16), preferred_element_type=F32)
    act = (0.5 * a * jax.nn.sigmoid(a)) * b
    o_ref[...] += jnp.dot(act.astype(BF16), wd_ref[...].astype(BF16), preferred_element_type=F32)

    if final_norm:
        @pl.when(j == pl.num_programs(1) - 1)
        def _():
            o_ref[...] = _rms(o_ref[...], fg_ref[...])


FFN_TF = 256
PROJ_TN = 1024


def _ffn(x, g, wg, wu, wd, layer, final_g=None, *, tm=1024):
    s, d = x.shape
    tf = FFN_TF
    f = wd.shape[1]
    final_norm = final_g is not None
    fg = final_g if final_norm else g
    return pl.pallas_call(
        functools.partial(_ffn_body, final_norm=final_norm),
        grid=(s // tm, f // tf),
        in_specs=[
            pl.BlockSpec((tm, d), lambda i, j: (i, 0)),
            pl.BlockSpec((1, d), lambda i, j: (0, 0)),
            _layer_spec(layer, d, tf, lambda i, j: (0, j)),
            _layer_spec(layer, d, tf, lambda i, j: (0, j)),
            _layer_spec(layer, tf, d, lambda i, j: (j, 0)),
            pl.BlockSpec((1, d), lambda i, j: (0, 0)),
        ],
        out_specs=pl.BlockSpec((tm, d), lambda i, j: (i, 0)),
        out_shape=jax.ShapeDtypeStruct((s, d), F32),
        scratch_shapes=[pltpu.VMEM((tm, d), BF16)],
        compiler_params=_params("parallel", "arbitrary"),
        name="ffn",
    )(x, g.reshape(1, d), wg, wu, wd, fg.reshape(1, d))


def _norm_proj_body(x_ref, g_ref, w_ref, cs_ref, o_ref, h_ref):
    @pl.when(pl.program_id(1) == 0)
    def _():
        h_ref[...] = _rms(x_ref[...], g_ref[...]).astype(BF16)

    acc = jnp.dot(h_ref[...], w_ref[...].astype(BF16), preferred_element_type=F32)
    o_ref[...] = (acc * cs_ref[...]).astype(o_ref.dtype)


def _norm_proj(x, g, w, layer, col_scale, *, tm=1024):
    s, d = x.shape
    tn = PROJ_TN
    n = w.shape[2]
    tm = min(tm, s)
    return pl.pallas_call(
        _norm_proj_body,
        grid=(s // tm, n // tn),
        in_specs=[
            pl.BlockSpec((tm, d), lambda i, j: (i, 0)),
            pl.BlockSpec((1, d), lambda i, j: (0, 0)),
            _layer_spec(layer, d, tn, lambda i, j: (0, j)),
            pl.BlockSpec((1, tn), lambda i, j: (0, j)),
        ],
        out_specs=pl.BlockSpec((tm, tn), lambda i, j: (i, j)),
        out_shape=jax.ShapeDtypeStruct((s, n), BF16),
        scratch_shapes=[pltpu.VMEM((tm, d), BF16)],
        compiler_params=_params("parallel", "arbitrary"),
        name="norm_proj",
    )(x, g.reshape(1, d), w, jnp.asarray(col_scale, F32).reshape(1, n))


def _col_scale(n, scaled):
    cs = np.ones((n,), np.float32)
    for lo, hi, val in scaled:
        cs[lo:hi] = val
    return cs


def _block_start(kb, t):
    return kb * t if isinstance(kb, int) else pl.multiple_of(kb * t, t)


ONES_ROWS = 16
UNROLL = 4


def _values_transposed(src_ref, dst_ref, seq, chunk):
    def body(c, carry):
        r0 = pl.multiple_of(c * chunk, chunk)
        dst_ref[0:HEAD_DIM, pl.ds(r0, chunk)] = src_ref[pl.ds(r0, chunk), :].astype(F32).T.astype(BF16)
        return carry
    lax.fori_loop(0, seq // chunk, body, 0)
    dst_ref[HEAD_DIM:, :] = jnp.ones((ONES_ROWS, seq), BF16)


def _diff_body(slopes_ref, lam_ref, q_ref, k_ref, v_ref, sg_ref, o_ref,
               vt_ref, rdiag_ref, a_ref, u0_ref, u1_ref, mx0_ref, mx1_ref, acc_ref,
               *, t, seq, out_scale):
    h = pl.program_id(0)
    i = pl.program_id(1)
    nk = seq // t
    dk = HEAD_DIM // 2
    reps = t // HEAD_DIM
    slope = slopes_ref[h]
    u_refs = (u0_ref, u1_ref)
    mx_refs = (mx0_ref, mx1_ref)

    @pl.when(i == 0)
    def _():
        _values_transposed(v_ref, vt_ref, seq, t)
        row = lax.broadcasted_iota(jnp.int32, (t, t), 0)
        col = lax.broadcasted_iota(jnp.int32, (t, t), 1)
        rdiag_ref[...] = -slope * jnp.abs(row - col).astype(F32)
        a = slope * lax.broadcasted_iota(jnp.int32, (t, HEAD_DIM), 0).astype(F32)
        a_ref[0] = a
        a_ref[1] = -a

    q = q_ref[...]
    lane = lax.broadcasted_iota(jnp.int32, q.shape, 1)
    zero = jnp.zeros_like(q)
    q_half = (jnp.where(lane < dk, q, zero), jnp.where(lane >= dk, q, zero))
    acc_ref[...] = jnp.zeros_like(acc_ref)
    q_off = slope * lax.broadcasted_iota(jnp.int32, (1, t), 1).astype(F32)

    def key_block(n):
        if isinstance(n, int) and n == 0:
            return i
        m = n - 1
        return jnp.where(n == 0, i, m + jnp.where(m >= i, 1, 0))

    def store_scores(u, slot, c):
        u_refs[slot][c] = u
        mx_refs[slot][c] = jnp.max(u, axis=0, keepdims=True)

    def diag_scores(slot):
        k = k_ref[pl.ds(_block_start(i, t), t), :]
        for c in range(2):
            store_scores(lax.dot_general(k, q_half[c], _NT, preferred_element_type=F32) + rdiag_ref[...], slot, c)

    def scores(n, slot):
        kb = key_block(n)
        k = k_ref[pl.ds(_block_start(kb, t), t), :]
        a = jnp.tile(a_ref[jnp.where(kb < i, 0, 1)], (1, reps))
        for c in range(2):
            store_scores(lax.dot_general(k, q_half[c], _NT, preferred_element_type=F32) + a, slot, c)

    def update(n, slot, carry):
        kb = key_block(n)
        vt = vt_ref[:, pl.ds(_block_start(kb, t), t)]
        off = slope * (jnp.abs(kb - i) * t).astype(F32)
        sign = jnp.where(kb < i, -1.0, jnp.where(kb > i, 1.0, 0.0))
        e = sign * q_off - off
        new = []
        for c in range(2):
            m_old = carry[c]
            m_new = jnp.maximum(m_old, mx_refs[slot][c] + e)
            alpha = jnp.exp2(m_old - m_new)
            p = jnp.exp2(u_refs[slot][c] - (m_new - e))
            acc_ref[c] = alpha * acc_ref[c] + jnp.dot(vt, p.astype(BF16), preferred_element_type=F32)
            new.append(m_new)
        return tuple(new)

    def group(j, carry):
        for s in range(UNROLL):
            n = UNROLL * j + s
            scores(n + 1, (s + 1) % 2)
            carry = update(n, s % 2, carry)
        return carry

    m_init = jnp.full((1, t), NEG_INF, F32)
    diag_scores(0)
    carry = lax.fori_loop(0, nk // UNROLL - 1, group, (m_init, m_init))
    for n in range(nk - UNROLL, nk):
        if n + 1 < nk:
            scores(n + 1, (n + 1) % 2)
        carry = update(n, n % 2, carry)

    lam = lam_ref[0]
    o0 = acc_ref[0, 0:HEAD_DIM] / acc_ref[0, HEAD_DIM:HEAD_DIM + 1]
    o1 = acc_ref[1, 0:HEAD_DIM] / acc_ref[1, HEAD_DIM:HEAD_DIM + 1]
    ot = o0 - lam * o1
    o = _rms(ot.T, sg_ref[...]) * out_scale
    o_ref[...] = o.astype(o_ref.dtype)


def _diff_attention(qkv, lam, subln, lambda_init, *, n_heads, q_col, k_col, v_col, t=512):
    seq = qkv.shape[0]
    assert UNROLL % 2 == 0 and (seq // t) % UNROLL == 0
    slopes = jnp.asarray(_alibi_slopes(n_heads) * np.float32(LOG2E))
    smem = pl.BlockSpec(memory_space=pltpu.SMEM)
    tile = pltpu.VMEM((2, t, t), F32)
    colmax = pltpu.VMEM((2, 1, t), F32)
    return pl.pallas_call(
        functools.partial(_diff_body, t=t, seq=seq, out_scale=1.0 - lambda_init),
        grid=(n_heads, seq // t),
        in_specs=[
            smem,
            smem,
            pl.BlockSpec((t, HEAD_DIM), lambda h, i: (i, q_col + h)),
            pl.BlockSpec((seq, HEAD_DIM), lambda h, i: (0, k_col + h)),
            pl.BlockSpec((seq, HEAD_DIM), lambda h, i: (0, v_col + h)),
            pl.BlockSpec((1, HEAD_DIM), lambda h, i: (0, 0)),
        ],
        out_specs=pl.BlockSpec((t, HEAD_DIM), lambda h, i: (i, h)),
        out_shape=jax.ShapeDtypeStruct((seq, n_heads * HEAD_DIM), BF16),
        scratch_shapes=[pltpu.VMEM((HEAD_DIM + ONES_ROWS, seq), BF16), pltpu.VMEM((t, t), F32),
                        pltpu.VMEM((2, t, HEAD_DIM), F32),
                        tile, tile, colmax, colmax, pltpu.VMEM((2, HEAD_DIM + ONES_ROWS, t), F32)],
        compiler_params=_params("parallel", "arbitrary"),
        name="diff_attn",
    )(slopes, lam.reshape(1).astype(F32), qkv, qkv, qkv, subln.reshape(1, HEAD_DIM))


NA_ROWS_Q = 4
NA_ROWS_K = 12


def _na_bias_table(rpb, rows):
    kh, kw, w = NA_KH, NA_KW, GRID_W
    n_heads = rpb.shape[0]
    nblk = rows // NA_ROWS_Q
    rpb = rpb.astype(F32) * LOG2E
    ext_idx = np.clip(np.arange(2 * w - 1) - (w - 1) + kw - 1, 0, 2 * kw - 2)
    lo = int(np.argmax(ext_idx > 0))
    hi = int(np.argmax(ext_idx == 2 * kw - 2))
    ext = jnp.concatenate([jnp.repeat(rpb[..., :1], lo - 1, axis=-1), rpb,
                           jnp.repeat(rpb[..., -1:], 2 * w - 1 - hi - 1, axis=-1)], axis=-1)
    tab = jnp.stack([ext[..., w - 1 - qc:2 * w - 1 - qc] for qc in range(w)], axis=-2)
    c = np.arange(w)
    c0 = np.clip(c - kw // 2, 0, w - kw)
    col_ok = (c[None, :] >= c0[:, None]) & (c[None, :] < c0[:, None] + kw)
    tab = jnp.where(col_ok, tab, NEG_INF)
    masked = jnp.full((n_heads, w, w), NEG_INF, F32)
    cases = []
    for blk in (0, 1, nblk - 1):
        ws = int(np.clip(NA_ROWS_Q * blk - kh // 2, 0, rows - NA_ROWS_K))
        q_rows = []
        for a in range(NA_ROWS_Q):
            r = NA_ROWS_Q * blk + a
            r0 = int(np.clip(r - kh // 2, 0, rows - kh))
            tiles = []
            for wi in range(NA_ROWS_K):
                kr = ws + wi
                tiles.append(tab[:, kr - r + kh - 1] if r0 <= kr < r0 + kh else masked)
            q_rows.append(jnp.concatenate(tiles, axis=-1))
        cases.append(jnp.concatenate(q_rows, axis=-2))
    return jnp.stack(cases, axis=1).swapaxes(-1, -2)


def _na_body(q_ref, k_ref, v_ref, b_ref, o_ref, vt_ref, u0_ref, u1_ref, *, rows):
    seq = rows * GRID_W
    nblk = rows // NA_ROWS_Q
    tq = NA_ROWS_Q * GRID_W
    nkeys = NA_ROWS_K * GRID_W
    u_refs = (u0_ref, u1_ref)
    _values_transposed(v_ref, vt_ref, seq, 512)

    def window(rb):
        ws = jnp.clip(NA_ROWS_Q * rb - NA_KH // 2, 0, rows - NA_ROWS_K)
        return pl.multiple_of(ws * GRID_W, NA_ROWS_Q * GRID_W)

    def scores(rb, slot):
        case = jnp.where(rb == 0, 0, jnp.where(rb == nblk - 1, 2, 1))
        k = k_ref[pl.ds(window(rb), nkeys), :]
        q = q_ref[pl.ds(_block_start(rb, tq), tq), :]
        u_refs[slot][...] = lax.dot_general(k, q, _NT, preferred_element_type=F32) + b_ref[case]

    def finish(rb, slot):
        u = u_refs[slot][...]
        p = jnp.exp2(u - jnp.max(u, axis=0, keepdims=True))
        vt = vt_ref[:, pl.ds(window(rb), nkeys)]
        acc = jnp.dot(vt, p.astype(BF16), preferred_element_type=F32)
        o = (acc[0:HEAD_DIM] / acc[HEAD_DIM:HEAD_DIM + 1]).T
        o_ref[pl.ds(_block_start(rb, tq), tq), :] = o.astype(o_ref.dtype)

    def pair(j, carry):
        rb = 2 * j
        scores(rb + 1, 1)
        finish(rb, 0)
        scores(rb + 2, 0)
        finish(rb + 1, 1)
        return carry

    scores(0, 0)
    lax.fori_loop(0, nblk // 2 - 1, pair, 0)
    scores(nblk - 1, 1)
    finish(nblk - 2, 0)
    finish(nblk - 1, 1)


def _neighborhood_attention(qkv, rpb, *, n_heads, q_col, k_col, v_col):
    seq = qkv.shape[0]
    rows = seq // GRID_W
    assert (rows // NA_ROWS_Q) % 2 == 0 and NA_ROWS_Q == NA_KH // 2
    tq = NA_ROWS_Q * GRID_W
    nkeys = NA_ROWS_K * GRID_W
    bias = _na_bias_table(rpb, rows)
    head_cols = lambda col: pl.BlockSpec((seq, HEAD_DIM), lambda h: (0, col + h))
    return pl.pallas_call(
        functools.partial(_na_body, rows=rows),
        grid=(n_heads,),
        in_specs=[
            head_cols(q_col),
            head_cols(k_col),
            head_cols(v_col),
            pl.BlockSpec((None, 3, nkeys, tq), lambda h: (h, 0, 0, 0)),
        ],
        out_specs=head_cols(0),
        out_shape=jax.ShapeDtypeStruct((seq, n_heads * HEAD_DIM), BF16),
        scratch_shapes=[pltpu.VMEM((HEAD_DIM + ONES_ROWS, seq), BF16),
                        pltpu.VMEM((nkeys, tq), F32), pltpu.VMEM((nkeys, tq), F32)],
        compiler_params=_params("arbitrary"),
        name="na_attn",
    )(qkv, qkv, qkv, bias)


def _dilated_band_tables(t, nb):
    rel = np.arange(t)[:, None] - np.arange(t)[None, :]
    dist, logc = [], []
    for dlt in range(-nb, nb + 1):
        ad = np.abs(rel + dlt * t)
        cnt = sum(((ad <= window // 2) & (ad % dil == 0)).astype(np.int64) for window, dil in DILATED_PAIRS)
        dist.append(np.where(cnt > 0, ad, 0))
        logc.append(np.where(cnt > 0, np.log2(np.maximum(cnt, 1)), NEG_INF))
    return np.stack(dist).astype(np.float32), np.stack(logc).astype(np.float32)


def _dil_body(slopes_ref, q_ref, k_ref, v_ref, dist_ref, logc_ref, o_ref, vt_ref, b_ref, u0_ref, u1_ref,
              acc_ref, *, t, seq, nb):
    h = pl.program_id(0)
    i = pl.program_id(1)
    nk = seq // t
    nsteps = 2 * nb + 1
    slope = slopes_ref[h]
    u_refs = (u0_ref, u1_ref)

    @pl.when(i == 0)
    def _():
        _values_transposed(v_ref, vt_ref, seq, t)
        for s in range(nsteps):
            b_ref[s] = logc_ref[s] - slope * dist_ref[s]
        b_ref[nsteps] = jnp.full((t, t), NEG_INF, F32)

    q = q_ref[...]
    acc_ref[...] = jnp.zeros_like(acc_ref)

    def scores(step, slot):
        kb = i - nb + step
        inside = (kb >= 0) & (kb < nk)
        k = k_ref[pl.ds(_block_start(jnp.clip(kb, 0, nk - 1), t), t), :]
        u = lax.dot_general(k, q, _NT, preferred_element_type=F32) + b_ref[jnp.where(inside, step, nsteps)]
        u_refs[slot][...] = u
        return jnp.max(u, axis=0, keepdims=True)

    def update(step, slot, mx, m_old):
        kb = jnp.clip(i - nb + step, 0, nk - 1)
        vt = vt_ref[:, pl.ds(_block_start(kb, t), t)]
        m_new = jnp.maximum(m_old, mx)
        alpha = jnp.exp2(m_old - m_new)
        p = jnp.exp2(u_refs[slot][...] - m_new)
        acc_ref[...] = alpha * acc_ref[...] + jnp.dot(vt, p.astype(BF16), preferred_element_type=F32)
        return m_new

    m = jnp.full((1, t), NEG_INF, F32)
    mx = scores(0, 0)
    for step in range(nsteps):
        mx_next = scores(step + 1, (step + 1) % 2) if step + 1 < nsteps else None
        m = update(step, step % 2, mx, m)
        mx = mx_next
    o_ref[...] = (acc_ref[0:HEAD_DIM] / acc_ref[HEAD_DIM:HEAD_DIM + 1]).T.astype(o_ref.dtype)


def _dilated_attention(qkv, *, n_heads, t=512):
    seq = qkv.shape[0]
    for window, dil in DILATED_PAIRS:
        assert dil & (dil - 1) == 0 and window % (2 * dil) == 0
    reach = max(window // 2 for window, _ in DILATED_PAIRS)
    nb = (reach - 1) // t + 1
    slopes = jnp.asarray(_alibi_slopes(n_heads) * np.float32(LOG2E))
    dist, logc = _dilated_band_tables(t, nb)
    table = pl.BlockSpec((2 * nb + 1, t, t), lambda h, i: (0, 0, 0), pipeline_mode=pl.Buffered(1))
    return pl.pallas_call(
        functools.partial(_dil_body, t=t, seq=seq, nb=nb),
        grid=(n_heads, seq // t),
        in_specs=[
            pl.BlockSpec(memory_space=pltpu.SMEM),
            pl.BlockSpec((t, HEAD_DIM), lambda h, i: (i, h)),
            pl.BlockSpec((seq, HEAD_DIM), lambda h, i: (0, n_heads + h)),
            pl.BlockSpec((seq, HEAD_DIM), lambda h, i: (0, 2 * n_heads + h)),
            table,
            table,
        ],
        out_specs=pl.BlockSpec((t, HEAD_DIM), lambda h, i: (i, h)),
        out_shape=jax.ShapeDtypeStruct((seq, n_heads * HEAD_DIM), BF16),
        scratch_shapes=[pltpu.VMEM((HEAD_DIM + ONES_ROWS, seq), BF16), pltpu.VMEM((2 * nb + 2, t, t), F32),
                        pltpu.VMEM((t, t), F32), pltpu.VMEM((t, t), F32),
                        pltpu.VMEM((HEAD_DIM + ONES_ROWS, t), F32)],
        compiler_params=_params("parallel", "arbitrary"),
        name="dilated_attn",
    )(slopes, qkv, qkv, qkv, jnp.asarray(dist), jnp.asarray(logc))


def _post_mix_body(a_ref, b_ref, wa_ref, wb_ref, x_ref, g_ref, wq_ref, kv_ref, wo_ref, o_ref, *, scale):
    y = x_ref[...] + jnp.dot(a_ref[...], wa_ref[...], preferred_element_type=F32)
    y = y + jnp.dot(b_ref[...], wb_ref[...], preferred_element_type=F32)
    h = _rms(y, g_ref[...]).astype(BF16)
    q = jnp.dot(h, wq_ref[...], preferred_element_type=F32).astype(BF16)
    width = N_HEADS_MEM * HEAD_DIM
    outs = []
    for hd in range(N_HEADS_MEM):
        lo, hi = hd * HEAD_DIM, (hd + 1) * HEAD_DIM
        s = lax.dot_general(q[:, lo:hi], kv_ref[:, lo:hi], _NT, preferred_element_type=F32) * scale
        m = jnp.max(s, axis=-1, keepdims=True)
        p = jnp.exp(s - m)
        l = jnp.sum(p, axis=-1, keepdims=True)
        o = jnp.dot(p.astype(BF16), kv_ref[:, width + lo:width + hi], preferred_element_type=F32) / l
        outs.append(o.astype(BF16))
    o = jnp.concatenate(outs, axis=-1)
    o_ref[...] = y + jnp.dot(o, wo_ref[...], preferred_element_type=F32)


def _post_mix(a, a_col, b, b_col, w_out, x, g, wq, kv, wo, layer, *, tm=512):
    s, d = x.shape
    kh = w_out.shape[1] // 2
    n_mem, kvw = kv.shape
    width = wq.shape[2]
    once = pl.Buffered(1)

    def resident(rows, cols, r):
        return pl.BlockSpec((None, rows, cols), lambda i: (layer, r, 0), pipeline_mode=once)

    return pl.pallas_call(
        functools.partial(_post_mix_body, scale=HEAD_DIM ** -0.5),
        grid=(s // tm,),
        in_specs=[
            pl.BlockSpec((tm, kh), lambda i: (i, a_col)),
            pl.BlockSpec((tm, kh), lambda i: (i, b_col)),
            resident(kh, d, 0),
            resident(kh, d, 1),
            pl.BlockSpec((tm, d), lambda i: (i, 0)),
            pl.BlockSpec((1, d), lambda i: (0, 0)),
            resident(d, width, 0),
            pl.BlockSpec((n_mem, kvw), lambda i: (0, 0), pipeline_mode=once),
            resident(width, d, 0),
        ],
        out_specs=pl.BlockSpec((tm, d), lambda i: (i, 0)),
        out_shape=jax.ShapeDtypeStruct((s, d), F32),
        compiler_params=_params("parallel"),
        name="post_mix",
    )(a, b, w_out, w_out, x, g.reshape(1, d), wq, kv, wo)


def kernel(x, mem, ffn1_norm, ffn1_w_gate, ffn1_w_up, ffn1_w_down, mix_norm, mix_w_in, mix_w_out, diff_lq1, diff_lk1, diff_lq2, diff_lk2, diff_subln, na_rpb, mem_q_norm, mem_kv_norm, mem_wq, mem_wkv, mem_wo, ffn2_norm, ffn2_w_gate, ffn2_w_up, ffn2_w_down, final_norm):
    batch, seq, d_model = x.shape
    assert batch == 1
    depth = ffn1_norm.shape[0]
    n_heads = d_model // HEAD_DIM
    in_proj = mix_w_in.shape[2]
    xs = x.reshape(seq, d_model)
    mem2 = mem.reshape(mem.shape[1], d_model)

    bf = lambda t: t.astype(BF16)
    w1g, w1u, w1d = ffn1_w_gate, ffn1_w_up, ffn1_w_down
    w2g, w2u, w2d = ffn2_w_gate, ffn2_w_up, ffn2_w_down
    w_in, w_out = mix_w_in, bf(mix_w_out)
    wq, wkv, wo = bf(mem_wq), mem_wkv, bf(mem_wo)

    hh = n_heads // 2
    half_w = hh * HEAD_DIM
    cs_even = _col_scale(in_proj, [(0, half_w, (HEAD_DIM // 2) ** -0.5 * LOG2E),
                                   (3 * half_w, 4 * half_w, HEAD_DIM ** -0.5 * LOG2E)])
    cs_odd = _col_scale(in_proj, [(0, d_model, HEAD_DIM ** -0.5 * LOG2E)])
    cs_kv = _col_scale(mem_wkv.shape[2], [])

    for i in range(depth):
        xs = _ffn(xs, ffn1_norm[i], w1g, w1u, w1d, i)
        if i % 2 == 0:
            e = i // 2
            qkv = _norm_proj(xs, mix_norm[i], w_in, i, cs_even)
            lambda_init = 0.8 - 0.6 * math.exp(-0.3 * i)
            lam = (jnp.exp(jnp.sum(diff_lq1[e].astype(F32) * diff_lk1[e].astype(F32)))
                   - jnp.exp(jnp.sum(diff_lq2[e].astype(F32) * diff_lk2[e].astype(F32)))
                   + lambda_init)
            oa = _diff_attention(qkv, lam, diff_subln[e], lambda_init, n_heads=hh,
                                 q_col=0, k_col=hh, v_col=2 * hh)
            ob = _neighborhood_attention(qkv, na_rpb[e], n_heads=hh,
                                         q_col=3 * hh, k_col=4 * hh, v_col=5 * hh)
            mix = (oa, 0, ob, 0)
        else:
            qkv = _norm_proj(xs, mix_norm[i], w_in, i, cs_odd)
            o = _dilated_attention(qkv, n_heads=n_heads)
            mix = (o, 0, o, 1)
        kv = _norm_proj(mem2, mem_kv_norm[i], wkv, i, cs_kv)
        xs = _post_mix(*mix, w_out, xs, mem_q_norm[i], wq, kv, wo, i)
        last = i == depth - 1
        xs = _ffn(xs, ffn2_norm[i], w2g, w2u, w2d, i, final_g=final_norm if last else None)
    return xs.reshape(batch, seq, d_model)
```

```python
import functools
import math

import numpy as np
import jax
import jax.numpy as jnp
from jax import lax
from jax.experimental import pallas as pl
from jax.experimental.pallas import tpu as pltpu

F32 = jnp.float32
BF16 = jnp.bfloat16

EPS = 1e-6
NEG_INF = -1e30
LOG2E = math.log2(math.e)
HEAD_DIM = 128
GRID_W = 64
NA_KH = 8
NA_KW = 16
N_HEADS_MEM = 4
DILATED_PAIRS = ((128, 1), (512, 4), (2048, 16))

VMEM_LIMIT = 52 * 1024 * 1024

_NT = (((1,), (1,)), ((), ()))


def _params(*sem):
    return pltpu.CompilerParams(dimension_semantics=sem, vmem_limit_bytes=VMEM_LIMIT)


def _rms(x, g):
    return x * lax.rsqrt(jnp.mean(x * x, axis=-1, keepdims=True) + EPS) * g


def _alibi_slopes(n):
    return np.exp2(-8.0 * np.arange(1, n + 1) / n).astype(np.float32)


def _layer_spec(layer, rows, cols, index_map):
    return pl.BlockSpec((None, rows, cols), lambda *g: (layer,) + tuple(index_map(*g)))


def _ffn_body(x_ref, g_ref, wg_ref, wu_ref, wd_ref, fg_ref, o_ref, h_ref, *, final_norm):
    j = pl.program_id(1)

    @pl.when(j == 0)
    def _():
        x = x_ref[...]
        h_ref[...] = _rms(x, g_ref[...]).astype(BF16)
        o_ref[...] = x

    h = h_ref[...]
    a = jnp.dot(h, wg_ref[...].astype(BF16), preferred_element_type=F32)
    b = jnp.dot(h, wu_ref[...].astype(BF16), preferred_element_type=F32)
    act = (0.5 * a * jax.nn.sigmoid(a)) * b
    o_ref[...] += jnp.dot(act.astype(BF16), wd_ref[...].astype(BF16), preferred_element_type=F32)

    if final_norm:
        @pl.when(j == pl.num_programs(1) - 1)
        def _():
            o_ref[...] = _rms(o_ref[...], fg_ref[...])


FFN_TF = 256
PROJ_TN = 1024


def _ffn(x, g, wg, wu, wd, layer, final_g=None, *, tm=1024):
    s, d = x.shape
    tf = FFN_TF
    f = wd.shape[1]
    final_norm = final_g is not None
    fg = final_g if final_norm else g
    return pl.pallas_call(
        functools.partial(_ffn_body, final_norm=final_norm),
        grid=(s // tm, f // tf),
        in_specs=[
            pl.BlockSpec((tm, d), lambda i, j: (i, 0)),
            pl.BlockSpec((1, d), lambda i, j: (0, 0)),
            _layer_spec(layer, d, tf, lambda i, j: (0, j)),
            _layer_spec(layer, d, tf, lambda i, j: (0, j)),
            _layer_spec(layer, tf, d, lambda i, j: (j, 0)),
            pl.BlockSpec((1, d), lambda i, j: (0, 0)),
        ],
        out_specs=pl.BlockSpec((tm, d), lambda i, j: (i, 0)),
        out_shape=jax.ShapeDtypeStruct((s, d), F32),
        scratch_shapes=[pltpu.VMEM((tm, d), BF16)],
        compiler_params=_params("parallel", "arbitrary"),
        name="ffn",
    )(x, g.reshape(1, d), wg, wu, wd, fg.reshape(1, d))


def _norm_proj_body(x_ref, g_ref, w_ref, cs_ref, o_ref, h_ref):
    @pl.when(pl.program_id(1) == 0)
    def _():
        h_ref[...] = _rms(x_ref[...], g_ref[...]).astype(BF16)

    acc = jnp.dot(h_ref[...], w_ref[...].astype(BF16), preferred_element_type=F32)
    o_ref[...] = (acc * cs_ref[...]).astype(o_ref.dtype)


def _norm_proj(x, g, w, layer, col_scale, *, tm=1024):
    s, d = x.shape
    tn = PROJ_TN
    n = w.shape[2]
    tm = min(tm, s)
    return pl.pallas_call(
        _norm_proj_body,
        grid=(s // tm, n // tn),
        in_specs=[
            pl.BlockSpec((tm, d), lambda i, j: (i, 0)),
            pl.BlockSpec((1, d), lambda i, j: (0, 0)),
            _layer_spec(layer, d, tn, lambda i, j: (0, j)),
            pl.BlockSpec((1, tn), lambda i, j: (0, j)),
        ],
        out_specs=pl.BlockSpec((tm, tn), lambda i, j: (i, j)),
        out_shape=jax.ShapeDtypeStruct((s, n), BF16),
        scratch_shapes=[pltpu.VMEM((tm, d), BF16)],
        compiler_params=_params("parallel", "arbitrary"),
        name="norm_proj",
    )(x, g.reshape(1, d), w, jnp.asarray(col_scale, F32).reshape(1, n))


def _col_scale(n, scaled):
    cs = np.ones((n,), np.float32)
    for lo, hi, val in scaled:
        cs[lo:hi] = val
    return cs


def _block_start(kb, t):
    return kb * t if isinstance(kb, int) else pl.multiple_of(kb * t, t)


ONES_ROWS = 16
UNROLL = 4
DIL_QBLOCKS = 2


def _values_transposed(src_ref, dst_ref, seq, chunk):
    def body(c, carry):
        r0 = pl.multiple_of(c * chunk, chunk)
        dst_ref[0:HEAD_DIM, pl.ds(r0, chunk)] = src_ref[pl.ds(r0, chunk), :].astype(F32).T.astype(BF16)
        return carry
    lax.fori_loop(0, seq // chunk, body, 0)
    dst_ref[HEAD_DIM:, :] = jnp.ones((ONES_ROWS, seq), BF16)


def _diff_body(slopes_ref, lam_ref, q_ref, k_ref, v_ref, sg_ref, o_ref,
               vt_ref, rdiag_ref, a_ref, u0_ref, u1_ref, mx0_ref, mx1_ref, acc_ref,
               *, t, seq, out_scale):
    h = pl.program_id(0)
    i = pl.program_id(1)
    nk = seq // t
    dk = HEAD_DIM // 2
    reps = t // HEAD_DIM
    slope = slopes_ref[h]
    u_refs = (u0_ref, u1_ref)
    mx_refs = (mx0_ref, mx1_ref)

    @pl.when(i == 0)
    def _():
        _values_transposed(v_ref, vt_ref, seq, t)
        row = lax.broadcasted_iota(jnp.int32, (t, t), 0)
        col = lax.broadcasted_iota(jnp.int32, (t, t), 1)
        rdiag_ref[...] = -slope * jnp.abs(row - col).astype(F32)
        a = slope * lax.broadcasted_iota(jnp.int32, (t, HEAD_DIM), 0).astype(F32)
        a_ref[0] = a
        a_ref[1] = -a

    q = q_ref[...]
    lane = lax.broadcasted_iota(jnp.int32, q.shape, 1)
    zero = jnp.zeros_like(q)
    q_half = (jnp.where(lane < dk, q, zero), jnp.where(lane >= dk, q, zero))
    acc_ref[...] = jnp.zeros_like(acc_ref)
    q_off = slope * lax.broadcasted_iota(jnp.int32, (1, t), 1).astype(F32)

    def key_block(n):
        if isinstance(n, int) and n == 0:
            return i
        m = n - 1
        return jnp.where(n == 0, i, m + jnp.where(m >= i, 1, 0))

    def store_scores(u, slot, c):
        u_refs[slot][c] = u
        mx_refs[slot][c] = jnp.max(u, axis=0, keepdims=True)

    def diag_scores(slot):
        k = k_ref[pl.ds(_block_start(i, t), t), :]
        for c in range(2):
            store_scores(lax.dot_general(k, q_half[c], _NT, preferred_element_type=F32) + rdiag_ref[...], slot, c)

    def scores(n, slot):
        kb = key_block(n)
        k = k_ref[pl.ds(_block_start(kb, t), t), :]
        a = jnp.tile(a_ref[jnp.where(kb < i, 0, 1)], (1, reps))
        for c in range(2):
            store_scores(lax.dot_general(k, q_half[c], _NT, preferred_element_type=F32) + a, slot, c)

    def update(n, slot, carry):
        kb = key_block(n)
        vt = vt_ref[:, pl.ds(_block_start(kb, t), t)]
        off = slope * (jnp.abs(kb - i) * t).astype(F32)
        sign = jnp.where(kb < i, -1.0, jnp.where(kb > i, 1.0, 0.0))
        e = sign * q_off - off
        new = []
        for c in range(2):
            m_old = carry[c]
            m_new = jnp.maximum(m_old, mx_refs[slot][c] + e)
            alpha = jnp.exp2(m_old - m_new)
            p = jnp.exp2(u_refs[slot][c] - (m_new - e))
            acc_ref[c] = alpha * acc_ref[c] + jnp.dot(vt, p.astype(BF16), preferred_element_type=F32)
            new.append(m_new)
        return tuple(new)

    def group(j, carry):
        for s in range(UNROLL):
            n = UNROLL * j + s
            scores(n + 1, (s + 1) % 2)
            carry = update(n, s % 2, carry)
        return carry

    m_init = jnp.full((1, t), NEG_INF, F32)
    diag_scores(0)
    carry = lax.fori_loop(0, nk // UNROLL - 1, group, (m_init, m_init))
    for n in range(nk - UNROLL, nk):
        if n + 1 < nk:
            scores(n + 1, (n + 1) % 2)
        carry = update(n, n % 2, carry)

    lam = lam_ref[0]
    o0 = acc_ref[0, 0:HEAD_DIM] / acc_ref[0, HEAD_DIM:HEAD_DIM + 1]
    o1 = acc_ref[1, 0:HEAD_DIM] / acc_ref[1, HEAD_DIM:HEAD_DIM + 1]
    ot = o0 - lam * o1
    o = _rms(ot.T, sg_ref[...]) * out_scale
    o_ref[...] = o.astype(o_ref.dtype)


def _diff_attention(qkv, lam, subln, lambda_init, *, n_heads, q_col, k_col, v_col, t=512):
    seq = qkv.shape[0]
    assert UNROLL % 2 == 0 and (seq // t) % UNROLL == 0
    slopes = jnp.asarray(_alibi_slopes(n_heads) * np.float32(LOG2E))
    smem = pl.BlockSpec(memory_space=pltpu.SMEM)
    tile = pltpu.VMEM((2, t, t), F32)
    colmax = pltpu.VMEM((2, 1, t), F32)
    return pl.pallas_call(
        functools.partial(_diff_body, t=t, seq=seq, out_scale=1.0 - lambda_init),
        grid=(n_heads, seq // t),
        in_specs=[
            smem,
            smem,
            pl.BlockSpec((t, HEAD_DIM), lambda h, i: (i, q_col + h)),
            pl.BlockSpec((seq, HEAD_DIM), lambda h, i: (0, k_col + h)),
            pl.BlockSpec((seq, HEAD_DIM), lambda h, i: (0, v_col + h)),
            pl.BlockSpec((1, HEAD_DIM), lambda h, i: (0, 0)),
        ],
        out_specs=pl.BlockSpec((t, HEAD_DIM), lambda h, i: (i, h)),
        out_shape=jax.ShapeDtypeStruct((seq, n_heads * HEAD_DIM), BF16),
        scratch_shapes=[pltpu.VMEM((HEAD_DIM + ONES_ROWS, seq), BF16), pltpu.VMEM((t, t), F32),
                        pltpu.VMEM((2, t, HEAD_DIM), F32),
                        tile, tile, colmax, colmax, pltpu.VMEM((2, HEAD_DIM + ONES_ROWS, t), F32)],
        compiler_params=_params("parallel", "arbitrary"),
        name="diff_attn",
    )(slopes, lam.reshape(1).astype(F32), qkv, qkv, qkv, subln.reshape(1, HEAD_DIM))


NA_ROWS_Q = 4
NA_ROWS_K = 12


def _na_bias_table(rpb, rows):
    kh, kw, w = NA_KH, NA_KW, GRID_W
    n_heads = rpb.shape[0]
    nblk = rows // NA_ROWS_Q
    rpb = rpb.astype(F32) * LOG2E
    ext_idx = np.clip(np.arange(2 * w - 1) - (w - 1) + kw - 1, 0, 2 * kw - 2)
    lo = int(np.argmax(ext_idx > 0))
    hi = int(np.argmax(ext_idx == 2 * kw - 2))
    ext = jnp.concatenate([jnp.repeat(rpb[..., :1], lo - 1, axis=-1), rpb,
                           jnp.repeat(rpb[..., -1:], 2 * w - 1 - hi - 1, axis=-1)], axis=-1)
    tab = jnp.stack([ext[..., w - 1 - qc:2 * w - 1 - qc] for qc in range(w)], axis=-2)
    c = np.arange(w)
    c0 = np.clip(c - kw // 2, 0, w - kw)
    col_ok = (c[None, :] >= c0[:, None]) & (c[None, :] < c0[:, None] + kw)
    tab = jnp.where(col_ok, tab, NEG_INF)
    masked = jnp.full((n_heads, w, w), NEG_INF, F32)
    cases = []
    for blk in (0, 1, nblk - 1):
        ws = int(np.clip(NA_ROWS_Q * blk - kh // 2, 0, rows - NA_ROWS_K))
        q_rows = []
        for a in range(NA_ROWS_Q):
            r = NA_ROWS_Q * blk + a
            r0 = int(np.clip(r - kh // 2, 0, rows - kh))
            tiles = []
            for wi in range(NA_ROWS_K):
                kr = ws + wi
                tiles.append(tab[:, kr - r + kh - 1] if r0 <= kr < r0 + kh else masked)
            q_rows.append(jnp.concatenate(tiles, axis=-1))
        cases.append(jnp.concatenate(q_rows, axis=-2))
    return jnp.stack(cases, axis=1).swapaxes(-1, -2)


def _na_body(q_ref, k_ref, v_ref, b_ref, o_ref, vt_ref, u0_ref, u1_ref, *, rows):
    seq = rows * GRID_W
    nblk = rows // NA_ROWS_Q
    tq = NA_ROWS_Q * GRID_W
    nkeys = NA_ROWS_K * GRID_W
    u_refs = (u0_ref, u1_ref)
    _values_transposed(v_ref, vt_ref, seq, 512)

    def window(rb):
        ws = jnp.clip(NA_ROWS_Q * rb - NA_KH // 2, 0, rows - NA_ROWS_K)
        return pl.multiple_of(ws * GRID_W, NA_ROWS_Q * GRID_W)

    def scores(rb, slot):
        case = jnp.where(rb == 0, 0, jnp.where(rb == nblk - 1, 2, 1))
        k = k_ref[pl.ds(window(rb), nkeys), :]
        q = q_ref[pl.ds(_block_start(rb, tq), tq), :]
        u_refs[slot][...] = lax.dot_general(k, q, _NT, preferred_element_type=F32) + b_ref[case]

    def finish(rb, slot):
        u = u_refs[slot][...]
        p = jnp.exp2(u - jnp.max(u, axis=0, keepdims=True))
        vt = vt_ref[:, pl.ds(window(rb), nkeys)]
        acc = jnp.dot(vt, p.astype(BF16), preferred_element_type=F32)
        o = (acc[0:HEAD_DIM] / acc[HEAD_DIM:HEAD_DIM + 1]).T
        o_ref[pl.ds(_block_start(rb, tq), tq), :] = o.astype(o_ref.dtype)

    def group(j, carry):
        for s in range(UNROLL):
            rb = UNROLL * j + s
            scores(rb + 1, (s + 1) % 2)
            finish(rb, s % 2)
        return carry

    scores(0, 0)
    lax.fori_loop(0, nblk // UNROLL - 1, group, 0)
    for rb in range(nblk - UNROLL, nblk):
        if rb + 1 < nblk:
            scores(rb + 1, (rb + 1) % 2)
        finish(rb, rb % 2)


def _neighborhood_attention(qkv, rpb, *, n_heads, q_col, k_col, v_col):
    seq = qkv.shape[0]
    rows = seq // GRID_W
    assert (rows // NA_ROWS_Q) % UNROLL == 0 and NA_ROWS_Q == NA_KH // 2
    tq = NA_ROWS_Q * GRID_W
    nkeys = NA_ROWS_K * GRID_W
    bias = _na_bias_table(rpb, rows)
    head_cols = lambda col: pl.BlockSpec((seq, HEAD_DIM), lambda h: (0, col + h))
    return pl.pallas_call(
        functools.partial(_na_body, rows=rows),
        grid=(n_heads,),
        in_specs=[
            head_cols(q_col),
            head_cols(k_col),
            head_cols(v_col),
            pl.BlockSpec((None, 3, nkeys, tq), lambda h: (h, 0, 0, 0)),
        ],
        out_specs=head_cols(0),
        out_shape=jax.ShapeDtypeStruct((seq, n_heads * HEAD_DIM), BF16),
        scratch_shapes=[pltpu.VMEM((HEAD_DIM + ONES_ROWS, seq), BF16),
                        pltpu.VMEM((nkeys, tq), F32), pltpu.VMEM((nkeys, tq), F32)],
        compiler_params=_params("arbitrary"),
        name="na_attn",
    )(qkv, qkv, qkv, bias)


def _dilated_band_tables(t, nb):
    rel = np.arange(t)[:, None] - np.arange(t)[None, :]
    dist, logc = [], []
    for dlt in range(-nb, nb + 1):
        ad = np.abs(rel + dlt * t)
        cnt = sum(((ad <= window // 2) & (ad % dil == 0)).astype(np.int64) for window, dil in DILATED_PAIRS)
        dist.append(np.where(cnt > 0, ad, 0))
        logc.append(np.where(cnt > 0, np.log2(np.maximum(cnt, 1)), NEG_INF))
    return np.stack(dist).astype(np.float32), np.stack(logc).astype(np.float32)


def _dil_body(slopes_ref, q_ref, k_ref, v_ref, dist_ref, logc_ref, o_ref, vt_ref, b_ref, u0_ref, u1_ref,
              acc_ref, *, t, seq, nb):
    h = pl.program_id(0)
    g = pl.program_id(1)
    nk = seq // t
    nsteps = 2 * nb + 1
    slope = slopes_ref[h]
    u_refs = (u0_ref, u1_ref)

    @pl.when(g == 0)
    def _():
        _values_transposed(v_ref, vt_ref, seq, t)
        for s in range(nsteps):
            b_ref[s] = logc_ref[s] - slope * dist_ref[s]
        b_ref[nsteps] = jnp.full((t, t), NEG_INF, F32)

    acc_ref[...] = jnp.zeros_like(acc_ref)

    def key_block(sub, step):
        return DIL_QBLOCKS * g + sub - nb + step

    def scores(sub, step, slot):
        kb = key_block(sub, step)
        inside = (kb >= 0) & (kb < nk)
        k = k_ref[pl.ds(_block_start(jnp.clip(kb, 0, nk - 1), t), t), :]
        q = q_ref[sub * t:(sub + 1) * t, :]
        u = lax.dot_general(k, q, _NT, preferred_element_type=F32) + b_ref[jnp.where(inside, step, nsteps)]
        u_refs[slot][...] = u
        return jnp.max(u, axis=0, keepdims=True)

    def update(sub, step, slot, mx, m_old):
        kb = jnp.clip(key_block(sub, step), 0, nk - 1)
        vt = vt_ref[:, pl.ds(_block_start(kb, t), t)]
        m_new = jnp.maximum(m_old, mx)
        alpha = jnp.exp2(m_old - m_new)
        p = jnp.exp2(u_refs[slot][...] - m_new)
        acc_ref[sub] = alpha * acc_ref[sub] + jnp.dot(vt, p.astype(BF16), preferred_element_type=F32)
        return m_new

    order = [(sub, step) for sub in range(DIL_QBLOCKS) for step in range(nsteps)]
    m = [jnp.full((1, t), NEG_INF, F32)] * DIL_QBLOCKS
    mx = scores(*order[0], 0)
    for n, (sub, step) in enumerate(order):
        mx_next = scores(*order[n + 1], (n + 1) % 2) if n + 1 < len(order) else None
        m[sub] = update(sub, step, n % 2, mx, m[sub])
        mx = mx_next
        if step == nsteps - 1:
            o = (acc_ref[sub, 0:HEAD_DIM] / acc_ref[sub, HEAD_DIM:HEAD_DIM + 1]).T
            o_ref[sub * t:(sub + 1) * t, :] = o.astype(o_ref.dtype)


def _dilated_attention(qkv, *, n_heads, t=512):
    seq = qkv.shape[0]
    for window, dil in DILATED_PAIRS:
        assert dil & (dil - 1) == 0 and window % (2 * dil) == 0
    reach = max(window // 2 for window, _ in DILATED_PAIRS)
    nb = (reach - 1) // t + 1
    slopes = jnp.asarray(_alibi_slopes(n_heads) * np.float32(LOG2E))
    dist, logc = _dilated_band_tables(t, nb)
    table = pl.BlockSpec((2 * nb + 1, t, t), lambda h, i: (0, 0, 0), pipeline_mode=pl.Buffered(1))
    return pl.pallas_call(
        functools.partial(_dil_body, t=t, seq=seq, nb=nb),
        grid=(n_heads, seq // (DIL_QBLOCKS * t)),
        in_specs=[
            pl.BlockSpec(memory_space=pltpu.SMEM),
            pl.BlockSpec((DIL_QBLOCKS * t, HEAD_DIM), lambda h, i: (i, h)),
            pl.BlockSpec((seq, HEAD_DIM), lambda h, i: (0, n_heads + h)),
            pl.BlockSpec((seq, HEAD_DIM), lambda h, i: (0, 2 * n_heads + h)),
            table,
            table,
        ],
        out_specs=pl.BlockSpec((DIL_QBLOCKS * t, HEAD_DIM), lambda h, i: (i, h)),
        out_shape=jax.ShapeDtypeStruct((seq, n_heads * HEAD_DIM), BF16),
        scratch_shapes=[pltpu.VMEM((HEAD_DIM + ONES_ROWS, seq), BF16), pltpu.VMEM((2 * nb + 2, t, t), F32),
                        pltpu.VMEM((t, t), F32), pltpu.VMEM((t, t), F32),
                        pltpu.VMEM((DIL_QBLOCKS, HEAD_DIM + ONES_ROWS, t), F32)],
        compiler_params=_params("parallel", "arbitrary"),
        name="dilated_attn",
    )(slopes, qkv, qkv, qkv, jnp.asarray(dist), jnp.asarray(logc))


def _post_mix_body(a_ref, b_ref, wa_ref, wb_ref, x_ref, g_ref, wq_ref, kv_ref, wo_ref, o_ref, *, scale):
    y = x_ref[...] + jnp.dot(a_ref[...], wa_ref[...], preferred_element_type=F32)
    y = y + jnp.dot(b_ref[...], wb_ref[...], preferred_element_type=F32)
    h = _rms(y, g_ref[...]).astype(BF16)
    q = jnp.dot(h, wq_ref[...], preferred_element_type=F32).astype(BF16)
    width = N_HEADS_MEM * HEAD_DIM
    outs = []
    for hd in range(N_HEADS_MEM):
        lo, hi = hd * HEAD_DIM, (hd + 1) * HEAD_DIM
        s = lax.dot_general(q[:, lo:hi], kv_ref[:, lo:hi], _NT, preferred_element_type=F32) * scale
        m = jnp.max(s, axis=-1, keepdims=True)
        p = jnp.exp(s - m)
        l = jnp.sum(p, axis=-1, keepdims=True)
        o = jnp.dot(p.astype(BF16), kv_ref[:, width + lo:width + hi], preferred_element_type=F32) / l
        outs.append(o.astype(BF16))
    o = jnp.concatenate(outs, axis=-1)
    o_ref[...] = y + jnp.dot(o, wo_ref[...], preferred_element_type=F32)


def _post_mix(a, a_col, b, b_col, w_out, x, g, wq, kv, wo, layer, *, tm=512):
    s, d = x.shape
    kh = w_out.shape[1] // 2
    n_mem, kvw = kv.shape
    width = wq.shape[2]
    once = pl.Buffered(1)

    def resident(rows, cols, r):
        return pl.BlockSpec((None, rows, cols), lambda i: (layer, r, 0), pipeline_mode=once)

    return pl.pallas_call(
        functools.partial(_post_mix_body, scale=HEAD_DIM ** -0.5),
        grid=(s // tm,),
        in_specs=[
            pl.BlockSpec((tm, kh), lambda i: (i, a_col)),
            pl.BlockSpec((tm, kh), lambda i: (i, b_col)),
            resident(kh, d, 0),
            resident(kh, d, 1),
            pl.BlockSpec((tm, d), lambda i: (i, 0)),
            pl.BlockSpec((1, d), lambda i: (0, 0)),
            resident(d, width, 0),
            pl.BlockSpec((n_mem, kvw), lambda i: (0, 0), pipeline_mode=once),
            resident(width, d, 0),
        ],
        out_specs=pl.BlockSpec((tm, d), lambda i: (i, 0)),
        out_shape=jax.ShapeDtypeStruct((s, d), F32),
        compiler_params=_params("parallel"),
        name="post_mix",
    )(a, b, w_out, w_out, x, g.reshape(1, d), wq, kv, wo)


def kernel(x, mem, ffn1_norm, ffn1_w_gate, ffn1_w_up, ffn1_w_down, mix_norm, mix_w_in, mix_w_out, diff_lq1, diff_lk1, diff_lq2, diff_lk2, diff_subln, na_rpb, mem_q_norm, mem_kv_norm, mem_wq, mem_wkv, mem_wo, ffn2_norm, ffn2_w_gate, ffn2_w_up, ffn2_w_down, final_norm):
    batch, seq, d_model = x.shape
    assert batch == 1
    depth = ffn1_norm.shape[0]
    n_heads = d_model // HEAD_DIM
    in_proj = mix_w_in.shape[2]
    xs = x.reshape(seq, d_model)
    mem2 = mem.reshape(mem.shape[1], d_model)

    bf = lambda t: t.astype(BF16)
    w1g, w1u, w1d = ffn1_w_gate, ffn1_w_up, ffn1_w_down
    w2g, w2u, w2d = ffn2_w_gate, ffn2_w_up, ffn2_w_down
    w_in, w_out = mix_w_in, bf(mix_w_out)
    wq, wkv, wo = bf(mem_wq), mem_wkv, bf(mem_wo)

    hh = n_heads // 2
    half_w = hh * HEAD_DIM
    cs_even = _col_scale(in_proj, [(0, half_w, (HEAD_DIM // 2) ** -0.5 * LOG2E),
                                   (3 * half_w, 4 * half_w, HEAD_DIM ** -0.5 * LOG2E)])
    cs_odd = _col_scale(in_proj, [(0, d_model, HEAD_DIM ** -0.5 * LOG2E)])
    cs_kv = _col_scale(mem_wkv.shape[2], [])

    for i in range(depth):
        xs = _ffn(xs, ffn1_norm[i], w1g, w1u, w1d, i)
        if i % 2 == 0:
            e = i // 2
            qkv = _norm_proj(xs, mix_norm[i], w_in, i, cs_even)
            lambda_init = 0.8 - 0.6 * math.exp(-0.3 * i)
            lam = (jnp.exp(jnp.sum(diff_lq1[e].astype(F32) * diff_lk1[e].astype(F32)))
                   - jnp.exp(jnp.sum(diff_lq2[e].astype(F32) * diff_lk2[e].astype(F32)))
                   + lambda_init)
            oa = _diff_attention(qkv, lam, diff_subln[e], lambda_init, n_heads=hh,
                                 q_col=0, k_col=hh, v_col=2 * hh)
            ob = _neighborhood_attention(qkv, na_rpb[e], n_heads=hh,
                                         q_col=3 * hh, k_col=4 * hh, v_col=5 * hh)
            mix = (oa, 0, ob, 0)
        else:
            qkv = _norm_proj(xs, mix_norm[i], w_in, i, cs_odd)
            o = _dilated_attention(qkv, n_heads=n_heads)
            mix = (o, 0, o, 1)
        kv = _norm_proj(mem2, mem_kv_norm[i], wkv, i, cs_kv)
        xs = _post_mix(*mix, w_out, xs, mem_q_norm[i], wq, kv, wo, i)
        last = i == depth - 1
        xs = _ffn(xs, ffn2_norm[i], w2g, w2u, w2d, i, final_g=final_norm if last else None)
    return xs.reshape(batch, seq, d_model)
```

```python
import functools
import math

import numpy as np
import jax
import jax.numpy as jnp
from jax import lax
from jax.experimental import pallas as pl
from jax.experimental.pallas import tpu as pltpu

F32 = jnp.float32
BF16 = jnp.bfloat16

EPS = 1e-6
NEG_INF = -1e30
LOG2E = math.log2(math.e)
HEAD_DIM = 128
GRID_W = 64
NA_KH = 8
NA_KW = 16
N_HEADS_MEM = 4
DILATED_PAIRS = ((128, 1), (512, 4), (2048, 16))

VMEM_LIMIT = 52 * 1024 * 1024

_NT = (((1,), (1,)), ((), ()))


def _params(*sem):
    return pltpu.CompilerParams(dimension_semantics=sem, vmem_limit_bytes=VMEM_LIMIT)


def _rms(x, g):
    return x * lax.rsqrt(jnp.mean(x * x, axis=-1, keepdims=True) + EPS) * g


def _alibi_slopes(n):
    return np.exp2(-8.0 * np.arange(1, n + 1) / n).astype(np.float32)


def _layer_spec(layer, rows, cols, index_map):
    return pl.BlockSpec((None, rows, cols), lambda *g: (layer,) + tuple(index_map(*g)))


def _ffn_body(x_ref, g_ref, wg_ref, wu_ref, wd_ref, fg_ref, o_ref, h_ref, *, final_norm):
    j = pl.program_id(1)

    @pl.when(j == 0)
    def _():
        x = x_ref[...]
        h_ref[...] = _rms(x, g_ref[...]).astype(BF16)
        o_ref[...] = x

    h = h_ref[...]
    a = jnp.dot(h, wg_ref[...].astype(BF16), preferred_element_type=F32)
    b = jnp.dot(h, wu_ref[...].astype(BF16), preferred_element_type=F32)
    act = (0.5 * a * jax.nn.sigmoid(a)) * b
    o_ref[...] += jnp.dot(act.astype(BF16), wd_ref[...].astype(BF16), preferred_element_type=F32)

    if final_norm:
        @pl.when(j == pl.num_programs(1) - 1)
        def _():
            o_ref[...] = _rms(o_ref[...], fg_ref[...])


FFN_TF = 256
PROJ_TN = 1024


def _ffn(x, g, wg, wu, wd, layer, final_g=None, *, tm=1024):
    s, d = x.shape
    tf = FFN_TF
    f = wd.shape[1]
    final_norm = final_g is not None
    fg = final_g if final_norm else g
    return pl.pallas_call(
        functools.partial(_ffn_body, final_norm=final_norm),
        grid=(s // tm, f // tf),
        in_specs=[
            pl.BlockSpec((tm, d), lambda i, j: (i, 0)),
            pl.BlockSpec((1, d), lambda i, j: (0, 0)),
            _layer_spec(layer, d, tf, lambda i, j: (0, j)),
            _layer_spec(layer, d, tf, lambda i, j: (0, j)),
            _layer_spec(layer, tf, d, lambda i, j: (j, 0)),
            pl.BlockSpec((1, d), lambda i, j: (0, 0)),
        ],
        out_specs=pl.BlockSpec((tm, d), lambda i, j: (i, 0)),
        out_shape=jax.ShapeDtypeStruct((s, d), F32),
        scratch_shapes=[pltpu.VMEM((tm, d), BF16)],
        compiler_params=_params("parallel", "arbitrary"),
        name="ffn",
    )(x, g.reshape(1, d), wg, wu, wd, fg.reshape(1, d))


def _norm_proj_body(x_ref, g_ref, w_ref, cs_ref, o_ref, h_ref):
    @pl.when(pl.program_id(1) == 0)
    def _():
        h_ref[...] = _rms(x_ref[...], g_ref[...]).astype(BF16)

    acc = jnp.dot(h_ref[...], w_ref[...].astype(BF16), preferred_element_type=F32)
    o_ref[...] = (acc * cs_ref[...]).astype(o_ref.dtype)


def _norm_proj(x, g, w, layer, col_scale, *, tm=1024):
    s, d = x.shape
    tn = PROJ_TN
    n = w.shape[2]
    tm = min(tm, s)
    return pl.pallas_call(
        _norm_proj_body,
        grid=(s // tm, n // tn),
        in_specs=[
            pl.BlockSpec((tm, d), lambda i, j: (i, 0)),
            pl.BlockSpec((1, d), lambda i, j: (0, 0)),
            _layer_spec(layer, d, tn, lambda i, j: (0, j)),
            pl.BlockSpec((1, tn), lambda i, j: (0, j)),
        ],
        out_specs=pl.BlockSpec((tm, tn), lambda i, j: (i, j)),
        out_shape=jax.ShapeDtypeStruct((s, n), BF16),
        scratch_shapes=[pltpu.VMEM((tm, d), BF16)],
        compiler_params=_params("parallel", "arbitrary"),
        name="norm_proj",
    )(x, g.reshape(1, d), w, jnp.asarray(col_scale, F32).reshape(1, n))


def _col_scale(n, scaled):
    cs = np.ones((n,), np.float32)
    for lo, hi, val in scaled:
        cs[lo:hi] = val
    return cs


def _block_start(kb, t):
    return kb * t if isinstance(kb, int) else pl.multiple_of(kb * t, t)


ONES_ROWS = 16
UNROLL = 4
NA_UNROLL = 8
DIL_QBLOCKS = 4


def _values_transposed(src_ref, dst_ref, seq, chunk):
    def body(c, carry):
        r0 = pl.multiple_of(c * chunk, chunk)
        dst_ref[0:HEAD_DIM, pl.ds(r0, chunk)] = src_ref[pl.ds(r0, chunk), :].astype(F32).T.astype(BF16)
        return carry
    lax.fori_loop(0, seq // chunk, body, 0)
    dst_ref[HEAD_DIM:, :] = jnp.ones((ONES_ROWS, seq), BF16)


def _diff_body(slopes_ref, lam_ref, q_ref, k_ref, v_ref, sg_ref, o_ref,
               vt_ref, rdiag_ref, a_ref, u0_ref, u1_ref, mx0_ref, mx1_ref, acc_ref,
               *, t, seq, out_scale):
    h = pl.program_id(0)
    i = pl.program_id(1)
    nk = seq // t
    dk = HEAD_DIM // 2
    reps = t // HEAD_DIM
    slope = slopes_ref[h]
    u_refs = (u0_ref, u1_ref)
    mx_refs = (mx0_ref, mx1_ref)

    @pl.when(i == 0)
    def _():
        _values_transposed(v_ref, vt_ref, seq, t)
        row = lax.broadcasted_iota(jnp.int32, (t, t), 0)
        col = lax.broadcasted_iota(jnp.int32, (t, t), 1)
        rdiag_ref[...] = -slope * jnp.abs(row - col).astype(F32)
        a = slope * lax.broadcasted_iota(jnp.int32, (t, HEAD_DIM), 0).astype(F32)
        a_ref[0] = a
        a_ref[1] = -a

    q = q_ref[...]
    lane = lax.broadcasted_iota(jnp.int32, q.shape, 1)
    zero = jnp.zeros_like(q)
    q_half = (jnp.where(lane < dk, q, zero), jnp.where(lane >= dk, q, zero))
    acc_ref[...] = jnp.zeros_like(acc_ref)
    q_off = slope * lax.broadcasted_iota(jnp.int32, (1, t), 1).astype(F32)

    def key_block(n):
        if isinstance(n, int) and n == 0:
            return i
        m = n - 1
        return jnp.where(n == 0, i, m + jnp.where(m >= i, 1, 0))

    def store_scores(u, slot, c):
        u_refs[slot][c] = u
        mx_refs[slot][c] = jnp.max(u, axis=0, keepdims=True)

    def diag_scores(slot):
        k = k_ref[pl.ds(_block_start(i, t), t), :]
        for c in range(2):
            store_scores(lax.dot_general(k, q_half[c], _NT, preferred_element_type=F32) + rdiag_ref[...], slot, c)

    def scores(n, slot):
        kb = key_block(n)
        k = k_ref[pl.ds(_block_start(kb, t), t), :]
        a = jnp.tile(a_ref[jnp.where(kb < i, 0, 1)], (1, reps))
        for c in range(2):
            store_scores(lax.dot_general(k, q_half[c], _NT, preferred_element_type=F32) + a, slot, c)

    def update(n, slot, carry):
        kb = key_block(n)
        vt = vt_ref[:, pl.ds(_block_start(kb, t), t)]
        off = slope * (jnp.abs(kb - i) * t).astype(F32)
        sign = jnp.where(kb < i, -1.0, jnp.where(kb > i, 1.0, 0.0))
        e = sign * q_off - off
        new = []
        for c in range(2):
            m_old = carry[c]
            m_new = jnp.maximum(m_old, mx_refs[slot][c] + e)
            alpha = jnp.exp2(m_old - m_new)
            p = jnp.exp2(u_refs[slot][c] - (m_new - e))
            acc_ref[c] = alpha * acc_ref[c] + jnp.dot(vt, p.astype(BF16), preferred_element_type=F32)
            new.append(m_new)
        return tuple(new)

    def group(j, carry):
        for s in range(UNROLL):
            n = UNROLL * j + s
            scores(n + 1, (s + 1) % 2)
            carry = update(n, s % 2, carry)
        return carry

    m_init = jnp.full((1, t), NEG_INF, F32)
    diag_scores(0)
    carry = lax.fori_loop(0, nk // UNROLL - 1, group, (m_init, m_init))
    for n in range(nk - UNROLL, nk):
        if n + 1 < nk:
            scores(n + 1, (n + 1) % 2)
        carry = update(n, n % 2, carry)

    lam = lam_ref[0]
    o0 = acc_ref[0, 0:HEAD_DIM] / acc_ref[0, HEAD_DIM:HEAD_DIM + 1]
    o1 = acc_ref[1, 0:HEAD_DIM] / acc_ref[1, HEAD_DIM:HEAD_DIM + 1]
    ot = o0 - lam * o1
    o = _rms(ot.T, sg_ref[...]) * out_scale
    o_ref[...] = o.astype(o_ref.dtype)


def _diff_attention(qkv, lam, subln, lambda_init, *, n_heads, q_col, k_col, v_col, t=512):
    seq = qkv.shape[0]
    assert UNROLL % 2 == 0 and (seq // t) % UNROLL == 0
    slopes = jnp.asarray(_alibi_slopes(n_heads) * np.float32(LOG2E))
    smem = pl.BlockSpec(memory_space=pltpu.SMEM)
    tile = pltpu.VMEM((2, t, t), F32)
    colmax = pltpu.VMEM((2, 1, t), F32)
    return pl.pallas_call(
        functools.partial(_diff_body, t=t, seq=seq, out_scale=1.0 - lambda_init),
        grid=(n_heads, seq // t),
        in_specs=[
            smem,
            smem,
            pl.BlockSpec((t, HEAD_DIM), lambda h, i: (i, q_col + h)),
            pl.BlockSpec((seq, HEAD_DIM), lambda h, i: (0, k_col + h)),
            pl.BlockSpec((seq, HEAD_DIM), lambda h, i: (0, v_col + h)),
            pl.BlockSpec((1, HEAD_DIM), lambda h, i: (0, 0)),
        ],
        out_specs=pl.BlockSpec((t, HEAD_DIM), lambda h, i: (i, h)),
        out_shape=jax.ShapeDtypeStruct((seq, n_heads * HEAD_DIM), BF16),
        scratch_shapes=[pltpu.VMEM((HEAD_DIM + ONES_ROWS, seq), BF16), pltpu.VMEM((t, t), F32),
                        pltpu.VMEM((2, t, HEAD_DIM), F32),
                        tile, tile, colmax, colmax, pltpu.VMEM((2, HEAD_DIM + ONES_ROWS, t), F32)],
        compiler_params=_params("parallel", "arbitrary"),
        name="diff_attn",
    )(slopes, lam.reshape(1).astype(F32), qkv, qkv, qkv, subln.reshape(1, HEAD_DIM))


NA_ROWS_Q = 4
NA_ROWS_K = 12


def _na_bias_table(rpb, rows):
    kh, kw, w = NA_KH, NA_KW, GRID_W
    n_heads = rpb.shape[0]
    nblk = rows // NA_ROWS_Q
    rpb = rpb.astype(F32) * LOG2E
    ext_idx = np.clip(np.arange(2 * w - 1) - (w - 1) + kw - 1, 0, 2 * kw - 2)
    lo = int(np.argmax(ext_idx > 0))
    hi = int(np.argmax(ext_idx == 2 * kw - 2))
    ext = jnp.concatenate([jnp.repeat(rpb[..., :1], lo - 1, axis=-1), rpb,
                           jnp.repeat(rpb[..., -1:], 2 * w - 1 - hi - 1, axis=-1)], axis=-1)
    tab = jnp.stack([ext[..., w - 1 - qc:2 * w - 1 - qc] for qc in range(w)], axis=-2)
    c = np.arange(w)
    c0 = np.clip(c - kw // 2, 0, w - kw)
    col_ok = (c[None, :] >= c0[:, None]) & (c[None, :] < c0[:, None] + kw)
    tab = jnp.where(col_ok, tab, NEG_INF)
    cases = []
    for blk in (0, 1, nblk - 1):
        ws = int(np.clip(NA_ROWS_Q * blk - kh // 2, 0, rows - NA_ROWS_K))
        q_rows = []
        for a in range(NA_ROWS_Q):
            r = NA_ROWS_Q * blk + a
            r0 = int(np.clip(r - kh // 2, 0, rows - kh))
            lo = r0 - ws
            assert 0 <= lo and lo + kh <= NA_ROWS_K
            first = r0 - r + kh - 1
            band = tab[:, first:first + kh].transpose(0, 2, 1, 3).reshape(n_heads, w, kh * w)
            q_rows.append(jnp.pad(band, ((0, 0), (0, 0), (lo * w, (NA_ROWS_K - lo - kh) * w)),
                                  constant_values=NEG_INF))
        cases.append(jnp.concatenate(q_rows, axis=-2))
    return jnp.stack(cases, axis=1).swapaxes(-1, -2)


def _na_body(q_ref, k_ref, v_ref, b_ref, o_ref, vt_ref, u0_ref, u1_ref, *, rows):
    seq = rows * GRID_W
    nblk = rows // NA_ROWS_Q
    tq = NA_ROWS_Q * GRID_W
    nkeys = NA_ROWS_K * GRID_W
    u_refs = (u0_ref, u1_ref)
    _values_transposed(v_ref, vt_ref, seq, 512)

    def window(rb):
        ws = jnp.clip(NA_ROWS_Q * rb - NA_KH // 2, 0, rows - NA_ROWS_K)
        return pl.multiple_of(ws * GRID_W, NA_ROWS_Q * GRID_W)

    def scores(rb, slot):
        case = jnp.where(rb == 0, 0, jnp.where(rb == nblk - 1, 2, 1))
        k = k_ref[pl.ds(window(rb), nkeys), :]
        q = q_ref[pl.ds(_block_start(rb, tq), tq), :]
        u_refs[slot][...] = lax.dot_general(k, q, _NT, preferred_element_type=F32) + b_ref[case]

    def finish(rb, slot):
        u = u_refs[slot][...]
        p = jnp.exp2(u - jnp.max(u, axis=0, keepdims=True))
        vt = vt_ref[:, pl.ds(window(rb), nkeys)]
        acc = jnp.dot(vt, p.astype(BF16), preferred_element_type=F32)
        o = (acc[0:HEAD_DIM] / acc[HEAD_DIM:HEAD_DIM + 1]).T
        o_ref[pl.ds(_block_start(rb, tq), tq), :] = o.astype(o_ref.dtype)

    def group(j, carry):
        for s in range(NA_UNROLL):
            rb = NA_UNROLL * j + s
            scores(rb + 1, (s + 1) % 2)
            finish(rb, s % 2)
        return carry

    scores(0, 0)
    lax.fori_loop(0, nblk // NA_UNROLL - 1, group, 0)
    for rb in range(nblk - NA_UNROLL, nblk):
        if rb + 1 < nblk:
            scores(rb + 1, (rb + 1) % 2)
        finish(rb, rb % 2)


def _neighborhood_attention(qkv, rpb, *, n_heads, q_col, k_col, v_col):
    seq = qkv.shape[0]
    rows = seq // GRID_W
    assert NA_UNROLL % 2 == 0 and (rows // NA_ROWS_Q) % NA_UNROLL == 0 and NA_ROWS_Q == NA_KH // 2
    tq = NA_ROWS_Q * GRID_W
    nkeys = NA_ROWS_K * GRID_W
    bias = _na_bias_table(rpb, rows)
    head_cols = lambda col: pl.BlockSpec((seq, HEAD_DIM), lambda h: (0, col + h))
    return pl.pallas_call(
        functools.partial(_na_body, rows=rows),
        grid=(n_heads,),
        in_specs=[
            head_cols(q_col),
            head_cols(k_col),
            head_cols(v_col),
            pl.BlockSpec((None, 3, nkeys, tq), lambda h: (h, 0, 0, 0)),
        ],
        out_specs=head_cols(0),
        out_shape=jax.ShapeDtypeStruct((seq, n_heads * HEAD_DIM), BF16),
        scratch_shapes=[pltpu.VMEM((HEAD_DIM + ONES_ROWS, seq), BF16),
                        pltpu.VMEM((nkeys, tq), F32), pltpu.VMEM((nkeys, tq), F32)],
        compiler_params=_params("arbitrary"),
        name="na_attn",
    )(qkv, qkv, qkv, bias)


def _dilated_band_tables(t, nb):
    rel = np.arange(t)[:, None] - np.arange(t)[None, :]
    dist, logc = [], []
    for dlt in range(-nb, nb + 1):
        ad = np.abs(rel + dlt * t)
        cnt = sum(((ad <= window // 2) & (ad % dil == 0)).astype(np.int64) for window, dil in DILATED_PAIRS)
        dist.append(np.where(cnt > 0, ad, 0))
        logc.append(np.where(cnt > 0, np.log2(np.maximum(cnt, 1)), NEG_INF))
    return np.stack(dist).astype(np.float32), np.stack(logc).astype(np.float32)


def _dil_body(slopes_ref, q_ref, k_ref, v_ref, dist_ref, logc_ref, o_ref, vt_ref, b_ref, u0_ref, u1_ref,
              acc_ref, *, t, seq, nb):
    h = pl.program_id(0)
    g = pl.program_id(1)
    nk = seq // t
    nsteps = 2 * nb + 1
    slope = slopes_ref[h]
    u_refs = (u0_ref, u1_ref)

    @pl.when(g == 0)
    def _():
        _values_transposed(v_ref, vt_ref, seq, t)
        for s in range(nsteps):
            b_ref[s] = logc_ref[s] - slope * dist_ref[s]
        b_ref[nsteps] = jnp.full((t, t), NEG_INF, F32)

    acc_ref[...] = jnp.zeros_like(acc_ref)

    def key_block(sub, step):
        return DIL_QBLOCKS * g + sub - nb + step

    def scores(sub, step, slot):
        kb = key_block(sub, step)
        inside = (kb >= 0) & (kb < nk)
        k = k_ref[pl.ds(_block_start(jnp.clip(kb, 0, nk - 1), t), t), :]
        q = q_ref[sub * t:(sub + 1) * t, :]
        u = lax.dot_general(k, q, _NT, preferred_element_type=F32) + b_ref[jnp.where(inside, step, nsteps)]
        u_refs[slot][...] = u
        return jnp.max(u, axis=0, keepdims=True)

    def update(sub, step, slot, mx, m_old):
        kb = jnp.clip(key_block(sub, step), 0, nk - 1)
        vt = vt_ref[:, pl.ds(_block_start(kb, t), t)]
        m_new = jnp.maximum(m_old, mx)
        alpha = jnp.exp2(m_old - m_new)
        p = jnp.exp2(u_refs[slot][...] - m_new)
        acc_ref[sub] = alpha * acc_ref[sub] + jnp.dot(vt, p.astype(BF16), preferred_element_type=F32)
        return m_new

    order = [(sub, step) for sub in range(DIL_QBLOCKS) for step in range(nsteps)]
    m = [jnp.full((1, t), NEG_INF, F32)] * DIL_QBLOCKS
    mx = scores(*order[0], 0)
    for n, (sub, step) in enumerate(order):
        mx_next = scores(*order[n + 1], (n + 1) % 2) if n + 1 < len(order) else None
        m[sub] = update(sub, step, n % 2, mx, m[sub])
        mx = mx_next
        if step == nsteps - 1:
            o = (acc_ref[sub, 0:HEAD_DIM] / acc_ref[sub, HEAD_DIM:HEAD_DIM + 1]).T
            o_ref[sub * t:(sub + 1) * t, :] = o.astype(o_ref.dtype)


def _dilated_attention(qkv, *, n_heads, t=512):
    seq = qkv.shape[0]
    for window, dil in DILATED_PAIRS:
        assert dil & (dil - 1) == 0 and window % (2 * dil) == 0
    reach = max(window // 2 for window, _ in DILATED_PAIRS)
    nb = (reach - 1) // t + 1
    slopes = jnp.asarray(_alibi_slopes(n_heads) * np.float32(LOG2E))
    dist, logc = _dilated_band_tables(t, nb)
    table = pl.BlockSpec((2 * nb + 1, t, t), lambda h, i: (0, 0, 0), pipeline_mode=pl.Buffered(1))
    return pl.pallas_call(
        functools.partial(_dil_body, t=t, seq=seq, nb=nb),
        grid=(n_heads, seq // (DIL_QBLOCKS * t)),
        in_specs=[
            pl.BlockSpec(memory_space=pltpu.SMEM),
            pl.BlockSpec((DIL_QBLOCKS * t, HEAD_DIM), lambda h, i: (i, h)),
            pl.BlockSpec((seq, HEAD_DIM), lambda h, i: (0, n_heads + h)),
            pl.BlockSpec((seq, HEAD_DIM), lambda h, i: (0, 2 * n_heads + h)),
            table,
            table,
        ],
        out_specs=pl.BlockSpec((DIL_QBLOCKS * t, HEAD_DIM), lambda h, i: (i, h)),
        out_shape=jax.ShapeDtypeStruct((seq, n_heads * HEAD_DIM), BF16),
        scratch_shapes=[pltpu.VMEM((HEAD_DIM + ONES_ROWS, seq), BF16), pltpu.VMEM((2 * nb + 2, t, t), F32),
                        pltpu.VMEM((t, t), F32), pltpu.VMEM((t, t), F32),
                        pltpu.VMEM((DIL_QBLOCKS, HEAD_DIM + ONES_ROWS, t), F32)],
        compiler_params=_params("parallel", "arbitrary"),
        name="dilated_attn",
    )(slopes, qkv, qkv, qkv, jnp.asarray(dist), jnp.asarray(logc))


def _post_mix_body(a_ref, b_ref, wa_ref, wb_ref, x_ref, g_ref, wq_ref, kv_ref, wo_ref, o_ref, *, scale):
    y = x_ref[...] + jnp.dot(a_ref[...], wa_ref[...], preferred_element_type=F32)
    y = y + jnp.dot(b_ref[...], wb_ref[...], preferred_element_type=F32)
    h = _rms(y, g_ref[...]).astype(BF16)
    q = jnp.dot(h, wq_ref[...], preferred_element_type=F32).astype(BF16)
    width = N_HEADS_MEM * HEAD_DIM
    outs = []
    for hd in range(N_HEADS_MEM):
        lo, hi = hd * HEAD_DIM, (hd + 1) * HEAD_DIM
        s = lax.dot_general(q[:, lo:hi], kv_ref[:, lo:hi], _NT, preferred_element_type=F32) * scale
        m = jnp.max(s, axis=-1, keepdims=True)
        p = jnp.exp(s - m)
        l = jnp.sum(p, axis=-1, keepdims=True)
        o = jnp.dot(p.astype(BF16), kv_ref[:, width + lo:width + hi], preferred_element_type=F32) / l
        outs.append(o.astype(BF16))
    o = jnp.concatenate(outs, axis=-1)
    o_ref[...] = y + jnp.dot(o, wo_ref[...], preferred_element_type=F32)


def _post_mix(a, a_col, b, b_col, w_out, x, g, wq, kv, wo, layer, *, tm=512):
    s, d = x.shape
    kh = w_out.shape[1] // 2
    n_mem, kvw = kv.shape
    width = wq.shape[2]
    once = pl.Buffered(1)

    def resident(rows, cols, r):
        return pl.BlockSpec((None, rows, cols), lambda i: (layer, r, 0), pipeline_mode=once)

    return pl.pallas_call(
        functools.partial(_post_mix_body, scale=HEAD_DIM ** -0.5),
        grid=(s // tm,),
        in_specs=[
            pl.BlockSpec((tm, kh), lambda i: (i, a_col)),
            pl.BlockSpec((tm, kh), lambda i: (i, b_col)),
            resident(kh, d, 0),
            resident(kh, d, 1),
            pl.BlockSpec((tm, d), lambda i: (i, 0)),
            pl.BlockSpec((1, d), lambda i: (0, 0)),
            resident(d, width, 0),
            pl.BlockSpec((n_mem, kvw), lambda i: (0, 0), pipeline_mode=once),
            resident(width, d, 0),
        ],
        out_specs=pl.BlockSpec((tm, d), lambda i: (i, 0)),
        out_shape=jax.ShapeDtypeStruct((s, d), F32),
        compiler_params=_params("parallel"),
        name="post_mix",
    )(a, b, w_out, w_out, x, g.reshape(1, d), wq, kv, wo)


def kernel(x, mem, ffn1_norm, ffn1_w_gate, ffn1_w_up, ffn1_w_down, mix_norm, mix_w_in, mix_w_out, diff_lq1, diff_lk1, diff_lq2, diff_lk2, diff_subln, na_rpb, mem_q_norm, mem_kv_norm, mem_wq, mem_wkv, mem_wo, ffn2_norm, ffn2_w_gate, ffn2_w_up, ffn2_w_down, final_norm):
    batch, seq, d_model = x.shape
    assert batch == 1
    depth = ffn1_norm.shape[0]
    n_heads = d_model // HEAD_DIM
    in_proj = mix_w_in.shape[2]
    xs = x.reshape(seq, d_model)
    mem2 = mem.reshape(mem.shape[1], d_model)

    bf = lambda t: t.astype(BF16)
    w1g, w1u, w1d = ffn1_w_gate, ffn1_w_up, ffn1_w_down
    w2g, w2u, w2d = ffn2_w_gate, ffn2_w_up, ffn2_w_down
    w_in, w_out = mix_w_in, bf(mix_w_out)
    wq, wkv, wo = bf(mem_wq), mem_wkv, bf(mem_wo)

    hh = n_heads // 2
    half_w = hh * HEAD_DIM
    cs_even = _col_scale(in_proj, [(0, half_w, (HEAD_DIM // 2) ** -0.5 * LOG2E),
                                   (3 * half_w, 4 * half_w, HEAD_DIM ** -0.5 * LOG2E)])
    cs_odd = _col_scale(in_proj, [(0, d_model, HEAD_DIM ** -0.5 * LOG2E)])
    cs_kv = _col_scale(mem_wkv.shape[2], [])

    for i in range(depth):
        xs = _ffn(xs, ffn1_norm[i], w1g, w1u, w1d, i)
        if i % 2 == 0:
            e = i // 2
            qkv = _norm_proj(xs, mix_norm[i], w_in, i, cs_even)
            lambda_init = 0.8 - 0.6 * math.exp(-0.3 * i)
            lam = (jnp.exp(jnp.sum(diff_lq1[e].astype(F32) * diff_lk1[e].astype(F32)))
                   - jnp.exp(jnp.sum(diff_lq2[e].astype(F32) * diff_lk2[e].astype(F32)))
                   + lambda_init)
            oa = _diff_attention(qkv, lam, diff_subln[e], lambda_init, n_heads=hh,
                                 q_col=0, k_col=hh, v_col=2 * hh)
            ob = _neighborhood_attention(qkv, na_rpb[e], n_heads=hh,
                                         q_col=3 * hh, k_col=4 * hh, v_col=5 * hh)
            mix = (oa, 0, ob, 0)
        else:
            qkv = _norm_proj(xs, mix_norm[i], w_in, i, cs_odd)
            o = _dilated_attention(qkv, n_heads=n_heads)
            mix = (o, 0, o, 1)
        kv = _norm_proj(mem2, mem_kv_norm[i], wkv, i, cs_kv)
        xs = _post_mix(*mix, w_out, xs, mem_q_norm[i], wq, kv, wo, i)
        last = i == depth - 1
        xs = _ffn(xs, ffn2_norm[i], w2g, w2u, w2d, i, final_g=final_norm if last else None)
    return xs.reshape(batch, seq, d_model)
```

```python
import functools
import math

import numpy as np
import jax
import jax.numpy as jnp
from jax import lax
from jax.experimental import pallas as pl
from jax.experimental.pallas import tpu as pltpu

F32 = jnp.float32
BF16 = jnp.bfloat16

EPS = 1e-6
NEG_INF = -1e30
LOG2E = math.log2(math.e)
HEAD_DIM = 128
GRID_W = 64
NA_KH = 8
NA_KW = 16
N_HEADS_MEM = 4
DILATED_PAIRS = ((128, 1), (512, 4), (2048, 16))

VMEM_LIMIT = 52 * 1024 * 1024

_NT = (((1,), (1,)), ((), ()))


def _params(*sem):
    return pltpu.CompilerParams(dimension_semantics=sem, vmem_limit_bytes=VMEM_LIMIT)


def _rms(x, g):
    return x * lax.rsqrt(jnp.mean(x * x, axis=-1, keepdims=True) + EPS) * g


def _alibi_slopes(n):
    return np.exp2(-8.0 * np.arange(1, n + 1) / n).astype(np.float32)


def _layer_spec(layer, rows, cols, index_map):
    return pl.BlockSpec((None, rows, cols), lambda *g: (layer,) + tuple(index_map(*g)))


def _ffn_body(x_ref, g_ref, wg_ref, wu_ref, wd_ref, fg_ref, o_ref, h_ref, *, final_norm):
    j = pl.program_id(1)

    @pl.when(j == 0)
    def _():
        x = x_ref[...]
        h_ref[...] = _rms(x, g_ref[...]).astype(BF16)
        o_ref[...] = x

    h = h_ref[...]
    a = jnp.dot(h, wg_ref[...].astype(BF16), preferred_element_type=F32)
    b = jnp.dot(h, wu_ref[...].astype(BF16), preferred_element_type=F32)
    act = (0.5 * a * jax.nn.sigmoid(a)) * b
    o_ref[...] += jnp.dot(act.astype(BF16), wd_ref[...].astype(BF16), preferred_element_type=F32)

    if final_norm:
        @pl.when(j == pl.num_programs(1) - 1)
        def _():
            o_ref[...] = _rms(o_ref[...], fg_ref[...])


FFN_TF = 256
PROJ_TN = 1024


def _ffn(x, g, wg, wu, wd, layer, final_g=None, *, tm=1024):
    s, d = x.shape
    tf = FFN_TF
    f = wd.shape[1]
    final_norm = final_g is not None
    fg = final_g if final_norm else g
    return pl.pallas_call(
        functools.partial(_ffn_body, final_norm=final_norm),
        grid=(s // tm, f // tf),
        in_specs=[
            pl.BlockSpec((tm, d), lambda i, j: (i, 0)),
            pl.BlockSpec((1, d), lambda i, j: (0, 0)),
            _layer_spec(layer, d, tf, lambda i, j: (0, j)),
            _layer_spec(layer, d, tf, lambda i, j: (0, j)),
            _layer_spec(layer, tf, d, lambda i, j: (j, 0)),
            pl.BlockSpec((1, d), lambda i, j: (0, 0)),
        ],
        out_specs=pl.BlockSpec((tm, d), lambda i, j: (i, 0)),
        out_shape=jax.ShapeDtypeStruct((s, d), F32),
        scratch_shapes=[pltpu.VMEM((tm, d), BF16)],
        compiler_params=_params("parallel", "arbitrary"),
        name="ffn",
    )(x, g.reshape(1, d), wg, wu, wd, fg.reshape(1, d))


def _norm_proj_body(x_ref, g_ref, w_ref, cs_ref, o_ref, h_ref):
    @pl.when(pl.program_id(1) == 0)
    def _():
        h_ref[...] = _rms(x_ref[...], g_ref[...]).astype(BF16)

    acc = jnp.dot(h_ref[...], w_ref[...].astype(BF16), preferred_element_type=F32)
    o_ref[...] = (acc * cs_ref[...]).astype(o_ref.dtype)


def _norm_proj(x, g, w, layer, col_scale, *, tm=1024):
    s, d = x.shape
    tn = PROJ_TN
    n = w.shape[2]
    tm = min(tm, s)
    return pl.pallas_call(
        _norm_proj_body,
        grid=(s // tm, n // tn),
        in_specs=[
            pl.BlockSpec((tm, d), lambda i, j: (i, 0)),
            pl.BlockSpec((1, d), lambda i, j: (0, 0)),
            _layer_spec(layer, d, tn, lambda i, j: (0, j)),
            pl.BlockSpec((1, tn), lambda i, j: (0, j)),
        ],
        out_specs=pl.BlockSpec((tm, tn), lambda i, j: (i, j)),
        out_shape=jax.ShapeDtypeStruct((s, n), BF16),
        scratch_shapes=[pltpu.VMEM((tm, d), BF16)],
        compiler_params=_params("parallel", "arbitrary"),
        name="norm_proj",
    )(x, g.reshape(1, d), w, jnp.asarray(col_scale, F32).reshape(1, n))


def _col_scale(n, scaled):
    cs = np.ones((n,), np.float32)
    for lo, hi, val in scaled:
        cs[lo:hi] = val
    return cs


def _block_start(kb, t):
    return kb * t if isinstance(kb, int) else pl.multiple_of(kb * t, t)


ONES_ROWS = 16
UNROLL = 4
NA_UNROLL = 8
DIL_QBLOCKS = 4
DIFF_QBLOCKS = 2


def _values_transposed(src_ref, dst_ref, seq, chunk):
    def body(c, carry):
        r0 = pl.multiple_of(c * chunk, chunk)
        dst_ref[0:HEAD_DIM, pl.ds(r0, chunk)] = src_ref[pl.ds(r0, chunk), :].astype(F32).T.astype(BF16)
        return carry
    lax.fori_loop(0, seq // chunk, body, 0)
    dst_ref[HEAD_DIM:, :] = jnp.ones((ONES_ROWS, seq), BF16)


def _diff_body(slopes_ref, lam_ref, q_ref, k_ref, v_ref, sg_ref, o_ref,
               vt_ref, rdiag_ref, a_ref, u0_ref, u1_ref, mx0_ref, mx1_ref, acc_ref,
               *, t, seq, out_scale):
    h = pl.program_id(0)
    g = pl.program_id(1)
    nk = seq // t
    dk = HEAD_DIM // 2
    reps = t // HEAD_DIM
    slope = slopes_ref[h]
    lam = lam_ref[0]
    u_refs = (u0_ref, u1_ref)
    mx_refs = (mx0_ref, mx1_ref)

    @pl.when(g == 0)
    def _():
        _values_transposed(v_ref, vt_ref, seq, t)
        row = lax.broadcasted_iota(jnp.int32, (t, t), 0)
        col = lax.broadcasted_iota(jnp.int32, (t, t), 1)
        rdiag_ref[...] = -slope * jnp.abs(row - col).astype(F32)
        a = slope * lax.broadcasted_iota(jnp.int32, (t, HEAD_DIM), 0).astype(F32)
        a_ref[0] = a
        a_ref[1] = -a

    acc_ref[...] = jnp.zeros_like(acc_ref)
    q_off = slope * lax.broadcasted_iota(jnp.int32, (1, t), 1).astype(F32)
    lane = lax.broadcasted_iota(jnp.int32, (t, HEAD_DIM), 1)

    def q_halves(sub):
        q = q_ref[sub * t:(sub + 1) * t, :]
        zero = jnp.zeros_like(q)
        return jnp.where(lane < dk, q, zero), jnp.where(lane >= dk, q, zero)

    q_half = [q_halves(sub) for sub in range(DIFF_QBLOCKS)]

    def key_block(i, n):
        if isinstance(n, int) and n == 0:
            return i
        m = n - 1
        return jnp.where(n == 0, i, m + jnp.where(m >= i, 1, 0))

    def store_scores(u, slot, c):
        u_refs[slot][c] = u
        mx_refs[slot][c] = jnp.max(u, axis=0, keepdims=True)

    def diag_scores(sub, slot):
        i = DIFF_QBLOCKS * g + sub
        k = k_ref[pl.ds(_block_start(i, t), t), :]
        for c in range(2):
            s = lax.dot_general(k, q_half[sub][c], _NT, preferred_element_type=F32)
            store_scores(s + rdiag_ref[...], slot, c)

    def scores(sub, n, slot):
        i = DIFF_QBLOCKS * g + sub
        kb = key_block(i, n)
        k = k_ref[pl.ds(_block_start(kb, t), t), :]
        a = jnp.tile(a_ref[jnp.where(kb < i, 0, 1)], (1, reps))
        for c in range(2):
            s = lax.dot_general(k, q_half[sub][c], _NT, preferred_element_type=F32)
            store_scores(s + a, slot, c)

    def update(sub, n, slot, carry):
        i = DIFF_QBLOCKS * g + sub
        kb = key_block(i, n)
        vt = vt_ref[:, pl.ds(_block_start(kb, t), t)]
        off = slope * (jnp.abs(kb - i) * t).astype(F32)
        sign = jnp.where(kb < i, -1.0, jnp.where(kb > i, 1.0, 0.0))
        e = sign * q_off - off
        new = []
        for c in range(2):
            m_old = carry[c]
            m_new = jnp.maximum(m_old, mx_refs[slot][c] + e)
            alpha = jnp.exp2(m_old - m_new)
            p = jnp.exp2(u_refs[slot][c] - (m_new - e))
            acc_ref[sub, c] = (alpha * acc_ref[sub, c]
                               + jnp.dot(vt, p.astype(BF16), preferred_element_type=F32))
            new.append(m_new)
        return tuple(new)

    def finish(sub):
        o0 = acc_ref[sub, 0, 0:HEAD_DIM] / acc_ref[sub, 0, HEAD_DIM:HEAD_DIM + 1]
        o1 = acc_ref[sub, 1, 0:HEAD_DIM] / acc_ref[sub, 1, HEAD_DIM:HEAD_DIM + 1]
        ot = o0 - lam * o1
        o = _rms(ot.T, sg_ref[...]) * out_scale
        o_ref[sub * t:(sub + 1) * t, :] = o.astype(o_ref.dtype)

    m_init = jnp.full((1, t), NEG_INF, F32)
    diag_scores(0, 0)
    for sub in range(DIFF_QBLOCKS):
        def group(j, carry, sub=sub):
            for s in range(UNROLL):
                n = UNROLL * j + s
                scores(sub, n + 1, (s + 1) % 2)
                carry = update(sub, n, s % 2, carry)
            return carry

        carry = lax.fori_loop(0, nk // UNROLL - 1, group, (m_init, m_init))
        for n in range(nk - UNROLL, nk):
            if n + 1 < nk:
                scores(sub, n + 1, (n + 1) % 2)
            elif sub + 1 < DIFF_QBLOCKS:
                diag_scores(sub + 1, (n + 1) % 2)
            carry = update(sub, n, n % 2, carry)
        finish(sub)


def _diff_attention(qkv, lam, subln, lambda_init, *, n_heads, q_col, k_col, v_col, t=512):
    seq = qkv.shape[0]
    assert UNROLL % 2 == 0 and (seq // t) % UNROLL == 0 and (seq // t) % DIFF_QBLOCKS == 0
    slopes = jnp.asarray(_alibi_slopes(n_heads) * np.float32(LOG2E))
    smem = pl.BlockSpec(memory_space=pltpu.SMEM)
    tile = pltpu.VMEM((2, t, t), F32)
    colmax = pltpu.VMEM((2, 1, t), F32)
    return pl.pallas_call(
        functools.partial(_diff_body, t=t, seq=seq, out_scale=1.0 - lambda_init),
        grid=(n_heads, seq // (DIFF_QBLOCKS * t)),
        in_specs=[
            smem,
            smem,
            pl.BlockSpec((DIFF_QBLOCKS * t, HEAD_DIM), lambda h, i: (i, q_col + h)),
            pl.BlockSpec((seq, HEAD_DIM), lambda h, i: (0, k_col + h)),
            pl.BlockSpec((seq, HEAD_DIM), lambda h, i: (0, v_col + h)),
            pl.BlockSpec((1, HEAD_DIM), lambda h, i: (0, 0)),
        ],
        out_specs=pl.BlockSpec((DIFF_QBLOCKS * t, HEAD_DIM), lambda h, i: (i, h)),
        out_shape=jax.ShapeDtypeStruct((seq, n_heads * HEAD_DIM), BF16),
        scratch_shapes=[pltpu.VMEM((HEAD_DIM + ONES_ROWS, seq), BF16), pltpu.VMEM((t, t), F32),
                        pltpu.VMEM((2, t, HEAD_DIM), F32),
                        tile, tile, colmax, colmax,
                        pltpu.VMEM((DIFF_QBLOCKS, 2, HEAD_DIM + ONES_ROWS, t), F32)],
        compiler_params=_params("parallel", "arbitrary"),
        name="diff_attn",
    )(slopes, lam.reshape(1).astype(F32), qkv, qkv, qkv, subln.reshape(1, HEAD_DIM))


NA_ROWS_Q = 4
NA_ROWS_K = 12


def _na_bias_table(rpb, rows):
    kh, kw, w = NA_KH, NA_KW, GRID_W
    n_heads = rpb.shape[0]
    nblk = rows // NA_ROWS_Q
    rpb = rpb.astype(F32) * LOG2E
    ext_idx = np.clip(np.arange(2 * w - 1) - (w - 1) + kw - 1, 0, 2 * kw - 2)
    lo = int(np.argmax(ext_idx > 0))
    hi = int(np.argmax(ext_idx == 2 * kw - 2))
    ext = jnp.concatenate([jnp.repeat(rpb[..., :1], lo - 1, axis=-1), rpb,
                           jnp.repeat(rpb[..., -1:], 2 * w - 1 - hi - 1, axis=-1)], axis=-1)
    tab = jnp.stack([ext[..., w - 1 - qc:2 * w - 1 - qc] for qc in range(w)], axis=-2)
    c = np.arange(w)
    c0 = np.clip(c - kw // 2, 0, w - kw)
    col_ok = (c[None, :] >= c0[:, None]) & (c[None, :] < c0[:, None] + kw)
    tab = jnp.where(col_ok, tab, NEG_INF)
    cases = []
    for blk in (0, 1, nblk - 1):
        ws = int(np.clip(NA_ROWS_Q * blk - kh // 2, 0, rows - NA_ROWS_K))
        q_rows = []
        for a in range(NA_ROWS_Q):
            r = NA_ROWS_Q * blk + a
            r0 = int(np.clip(r - kh // 2, 0, rows - kh))
            lo = r0 - ws
            assert 0 <= lo and lo + kh <= NA_ROWS_K
            first = r0 - r + kh - 1
            band = tab[:, first:first + kh].transpose(0, 2, 1, 3).reshape(n_heads, w, kh * w)
            q_rows.append(jnp.pad(band, ((0, 0), (0, 0), (lo * w, (NA_ROWS_K - lo - kh) * w)),
                                  constant_values=NEG_INF))
        cases.append(jnp.concatenate(q_rows, axis=-2))
    return jnp.stack(cases, axis=1).swapaxes(-1, -2)


def _na_body(q_ref, k_ref, v_ref, b_ref, o_ref, vt_ref, u0_ref, u1_ref, *, rows):
    seq = rows * GRID_W
    nblk = rows // NA_ROWS_Q
    tq = NA_ROWS_Q * GRID_W
    nkeys = NA_ROWS_K * GRID_W
    u_refs = (u0_ref, u1_ref)
    _values_transposed(v_ref, vt_ref, seq, 512)

    def window(rb):
        ws = jnp.clip(NA_ROWS_Q * rb - NA_KH // 2, 0, rows - NA_ROWS_K)
        return pl.multiple_of(ws * GRID_W, NA_ROWS_Q * GRID_W)

    def scores(rb, slot):
        case = jnp.where(rb == 0, 0, jnp.where(rb == nblk - 1, 2, 1))
        k = k_ref[pl.ds(window(rb), nkeys), :]
        q = q_ref[pl.ds(_block_start(rb, tq), tq), :]
        u_refs[slot][...] = lax.dot_general(k, q, _NT, preferred_element_type=F32) + b_ref[case]

    def finish(rb, slot):
        u = u_refs[slot][...]
        p = jnp.exp2(u - jnp.max(u, axis=0, keepdims=True))
        vt = vt_ref[:, pl.ds(window(rb), nkeys)]
        acc = jnp.dot(vt, p.astype(BF16), preferred_element_type=F32)
        o = (acc[0:HEAD_DIM] / acc[HEAD_DIM:HEAD_DIM + 1]).T
        o_ref[pl.ds(_block_start(rb, tq), tq), :] = o.astype(o_ref.dtype)

    def group(j, carry):
        for s in range(NA_UNROLL):
            rb = NA_UNROLL * j + s
            scores(rb + 1, (s + 1) % 2)
            finish(rb, s % 2)
        return carry

    scores(0, 0)
    lax.fori_loop(0, nblk // NA_UNROLL - 1, group, 0)
    for rb in range(nblk - NA_UNROLL, nblk):
        if rb + 1 < nblk:
            scores(rb + 1, (rb + 1) % 2)
        finish(rb, rb % 2)


def _neighborhood_attention(qkv, rpb, *, n_heads, q_col, k_col, v_col):
    seq = qkv.shape[0]
    rows = seq // GRID_W
    assert NA_UNROLL % 2 == 0 and (rows // NA_ROWS_Q) % NA_UNROLL == 0 and NA_ROWS_Q == NA_KH // 2
    tq = NA_ROWS_Q * GRID_W
    nkeys = NA_ROWS_K * GRID_W
    bias = _na_bias_table(rpb, rows)
    head_cols = lambda col: pl.BlockSpec((seq, HEAD_DIM), lambda h: (0, col + h))
    return pl.pallas_call(
        functools.partial(_na_body, rows=rows),
        grid=(n_heads,),
        in_specs=[
            head_cols(q_col),
            head_cols(k_col),
            head_cols(v_col),
            pl.BlockSpec((None, 3, nkeys, tq), lambda h: (h, 0, 0, 0)),
        ],
        out_specs=head_cols(0),
        out_shape=jax.ShapeDtypeStruct((seq, n_heads * HEAD_DIM), BF16),
        scratch_shapes=[pltpu.VMEM((HEAD_DIM + ONES_ROWS, seq), BF16),
                        pltpu.VMEM((nkeys, tq), F32), pltpu.VMEM((nkeys, tq), F32)],
        compiler_params=_params("arbitrary"),
        name="na_attn",
    )(qkv, qkv, qkv, bias)


def _dilated_band_tables(t, nb):
    rel = np.arange(t)[:, None] - np.arange(t)[None, :]
    dist, logc = [], []
    for dlt in range(-nb, nb + 1):
        ad = np.abs(rel + dlt * t)
        cnt = sum(((ad <= window // 2) & (ad % dil == 0)).astype(np.int64) for window, dil in DILATED_PAIRS)
        dist.append(np.where(cnt > 0, ad, 0))
        logc.append(np.where(cnt > 0, np.log2(np.maximum(cnt, 1)), NEG_INF))
    return np.stack(dist).astype(np.float32), np.stack(logc).astype(np.float32)


def _dil_body(slopes_ref, q_ref, k_ref, v_ref, dist_ref, logc_ref, o_ref, vt_ref, b_ref, u0_ref, u1_ref,
              acc_ref, *, t, seq, nb):
    h = pl.program_id(0)
    g = pl.program_id(1)
    nk = seq // t
    nsteps = 2 * nb + 1
    slope = slopes_ref[h]
    u_refs = (u0_ref, u1_ref)

    @pl.when(g == 0)
    def _():
        _values_transposed(v_ref, vt_ref, seq, t)
        for s in range(nsteps):
            b_ref[s] = logc_ref[s] - slope * dist_ref[s]
        b_ref[nsteps] = jnp.full((t, t), NEG_INF, F32)

    acc_ref[...] = jnp.zeros_like(acc_ref)

    def key_block(sub, step):
        return DIL_QBLOCKS * g + sub - nb + step

    def scores(sub, step, slot):
        kb = key_block(sub, step)
        inside = (kb >= 0) & (kb < nk)
        k = k_ref[pl.ds(_block_start(jnp.clip(kb, 0, nk - 1), t), t), :]
        q = q_ref[sub * t:(sub + 1) * t, :]
        u = lax.dot_general(k, q, _NT, preferred_element_type=F32) + b_ref[jnp.where(inside, step, nsteps)]
        u_refs[slot][...] = u
        return jnp.max(u, axis=0, keepdims=True)

    def update(sub, step, slot, mx, m_old):
        kb = jnp.clip(key_block(sub, step), 0, nk - 1)
        vt = vt_ref[:, pl.ds(_block_start(kb, t), t)]
        m_new = jnp.maximum(m_old, mx)
        alpha = jnp.exp2(m_old - m_new)
        p = jnp.exp2(u_refs[slot][...] - m_new)
        acc_ref[sub] = alpha * acc_ref[sub] + jnp.dot(vt, p.astype(BF16), preferred_element_type=F32)
        return m_new

    order = [(sub, step) for sub in range(DIL_QBLOCKS) for step in range(nsteps)]
    m = [jnp.full((1, t), NEG_INF, F32)] * DIL_QBLOCKS
    mx = scores(*order[0], 0)
    for n, (sub, step) in enumerate(order):
        mx_next = scores(*order[n + 1], (n + 1) % 2) if n + 1 < len(order) else None
        m[sub] = update(sub, step, n % 2, mx, m[sub])
        mx = mx_next
        if step == nsteps - 1:
            o = (acc_ref[sub, 0:HEAD_DIM] / acc_ref[sub, HEAD_DIM:HEAD_DIM + 1]).T
            o_ref[sub * t:(sub + 1) * t, :] = o.astype(o_ref.dtype)


def _dilated_attention(qkv, *, n_heads, t=512):
    seq = qkv.shape[0]
    for window, dil in DILATED_PAIRS:
        assert dil & (dil - 1) == 0 and window % (2 * dil) == 0
    reach = max(window // 2 for window, _ in DILATED_PAIRS)
    nb = (reach - 1) // t + 1
    slopes = jnp.asarray(_alibi_slopes(n_heads) * np.float32(LOG2E))
    dist, logc = _dilated_band_tables(t, nb)
    table = pl.BlockSpec((2 * nb + 1, t, t), lambda h, i: (0, 0, 0), pipeline_mode=pl.Buffered(1))
    return pl.pallas_call(
        functools.partial(_dil_body, t=t, seq=seq, nb=nb),
        grid=(n_heads, seq // (DIL_QBLOCKS * t)),
        in_specs=[
            pl.BlockSpec(memory_space=pltpu.SMEM),
            pl.BlockSpec((DIL_QBLOCKS * t, HEAD_DIM), lambda h, i: (i, h)),
            pl.BlockSpec((seq, HEAD_DIM), lambda h, i: (0, n_heads + h)),
            pl.BlockSpec((seq, HEAD_DIM), lambda h, i: (0, 2 * n_heads + h)),
            table,
            table,
        ],
        out_specs=pl.BlockSpec((DIL_QBLOCKS * t, HEAD_DIM), lambda h, i: (i, h)),
        out_shape=jax.ShapeDtypeStruct((seq, n_heads * HEAD_DIM), BF16),
        scratch_shapes=[pltpu.VMEM((HEAD_DIM + ONES_ROWS, seq), BF16), pltpu.VMEM((2 * nb + 2, t, t), F32),
                        pltpu.VMEM((t, t), F32), pltpu.VMEM((t, t), F32),
                        pltpu.VMEM((DIL_QBLOCKS, HEAD_DIM + ONES_ROWS, t), F32)],
        compiler_params=_params("parallel", "arbitrary"),
        name="dilated_attn",
    )(slopes, qkv, qkv, qkv, jnp.asarray(dist), jnp.asarray(logc))


def _post_mix_body(a_ref, b_ref, wa_ref, wb_ref, x_ref, g_ref, wq_ref, kv_ref, wo_ref, o_ref, *, scale):
    y = x_ref[...] + jnp.dot(a_ref[...], wa_ref[...], preferred_element_type=F32)
    y = y + jnp.dot(b_ref[...], wb_ref[...], preferred_element_type=F32)
    h = _rms(y, g_ref[...]).astype(BF16)
    q = jnp.dot(h, wq_ref[...], preferred_element_type=F32).astype(BF16)
    width = N_HEADS_MEM * HEAD_DIM
    outs = []
    for hd in range(N_HEADS_MEM):
        lo, hi = hd * HEAD_DIM, (hd + 1) * HEAD_DIM
        s = lax.dot_general(q[:, lo:hi], kv_ref[:, lo:hi], _NT, preferred_element_type=F32) * scale
        m = jnp.max(s, axis=-1, keepdims=True)
        p = jnp.exp(s - m)
        l = jnp.sum(p, axis=-1, keepdims=True)
        o = jnp.dot(p.astype(BF16), kv_ref[:, width + lo:width + hi], preferred_element_type=F32) / l
        outs.append(o.astype(BF16))
    o = jnp.concatenate(outs, axis=-1)
    o_ref[...] = y + jnp.dot(o, wo_ref[...], preferred_element_type=F32)


def _post_mix(a, a_col, b, b_col, w_out, x, g, wq, kv, wo, layer, *, tm=512):
    s, d = x.shape
    kh = w_out.shape[1] // 2
    n_mem, kvw = kv.shape
    width = wq.shape[2]
    once = pl.Buffered(1)

    def resident(rows, cols, r):
        return pl.BlockSpec((None, rows, cols), lambda i: (layer, r, 0), pipeline_mode=once)

    return pl.pallas_call(
        functools.partial(_post_mix_body, scale=HEAD_DIM ** -0.5),
        grid=(s // tm,),
        in_specs=[
            pl.BlockSpec((tm, kh), lambda i: (i, a_col)),
            pl.BlockSpec((tm, kh), lambda i: (i, b_col)),
            resident(kh, d, 0),
            resident(kh, d, 1),
            pl.BlockSpec((tm, d), lambda i: (i, 0)),
            pl.BlockSpec((1, d), lambda i: (0, 0)),
            resident(d, width, 0),
            pl.BlockSpec((n_mem, kvw), lambda i: (0, 0), pipeline_mode=once),
            resident(width, d, 0),
        ],
        out_specs=pl.BlockSpec((tm, d), lambda i: (i, 0)),
        out_shape=jax.ShapeDtypeStruct((s, d), F32),
        compiler_params=_params("parallel"),
        name="post_mix",
    )(a, b, w_out, w_out, x, g.reshape(1, d), wq, kv, wo)


def kernel(x, mem, ffn1_norm, ffn1_w_gate, ffn1_w_up, ffn1_w_down, mix_norm, mix_w_in, mix_w_out, diff_lq1, diff_lk1, diff_lq2, diff_lk2, diff_subln, na_rpb, mem_q_norm, mem_kv_norm, mem_wq, mem_wkv, mem_wo, ffn2_norm, ffn2_w_gate, ffn2_w_up, ffn2_w_down, final_norm):
    batch, seq, d_model = x.shape
    assert batch == 1
    depth = ffn1_norm.shape[0]
    n_heads = d_model // HEAD_DIM
    in_proj = mix_w_in.shape[2]
    xs = x.reshape(seq, d_model)
    mem2 = mem.reshape(mem.shape[1], d_model)

    bf = lambda t: t.astype(BF16)
    w1g, w1u, w1d = ffn1_w_gate, ffn1_w_up, ffn1_w_down
    w2g, w2u, w2d = ffn2_w_gate, ffn2_w_up, ffn2_w_down
    w_in, w_out = mix_w_in, bf(mix_w_out)
    wq, wkv, wo = bf(mem_wq), mem_wkv, bf(mem_wo)

    hh = n_heads // 2
    half_w = hh * HEAD_DIM
    cs_even = _col_scale(in_proj, [(0, half_w, (HEAD_DIM // 2) ** -0.5 * LOG2E),
                                   (3 * half_w, 4 * half_w, HEAD_DIM ** -0.5 * LOG2E)])
    cs_odd = _col_scale(in_proj, [(0, d_model, HEAD_DIM ** -0.5 * LOG2E)])
    cs_kv = _col_scale(mem_wkv.shape[2], [])

    for i in range(depth):
        xs = _ffn(xs, ffn1_norm[i], w1g, w1u, w1d, i)
        if i % 2 == 0:
            e = i // 2
            qkv = _norm_proj(xs, mix_norm[i], w_in, i, cs_even)
            lambda_init = 0.8 - 0.6 * math.exp(-0.3 * i)
            lam = (jnp.exp(jnp.sum(diff_lq1[e].astype(F32) * diff_lk1[e].astype(F32)))
                   - jnp.exp(jnp.sum(diff_lq2[e].astype(F32) * diff_lk2[e].astype(F32)))
                   + lambda_init)
            oa = _diff_attention(qkv, lam, diff_subln[e], lambda_init, n_heads=hh,
                                 q_col=0, k_col=hh, v_col=2 * hh)
            ob = _neighborhood_attention(qkv, na_rpb[e], n_heads=hh,
                                         q_col=3 * hh, k_col=4 * hh, v_col=5 * hh)
            mix = (oa, 0, ob, 0)
        else:
            qkv = _norm_proj(xs, mix_norm[i], w_in, i, cs_odd)
            o = _dilated_attention(qkv, n_heads=n_heads)
            mix = (o, 0, o, 1)
        kv = _norm_proj(mem2, mem_kv_norm[i], wkv, i, cs_kv)
        xs = _post_mix(*mix, w_out, xs, mem_q_norm[i], wq, kv, wo, i)
        last = i == depth - 1
        xs = _ffn(xs, ffn2_norm[i], w2g, w2u, w2d, i, final_g=final_norm if last else None)
    return xs.reshape(batch, seq, d_model)
```

```python
import functools
import math

import numpy as np
import jax
import jax.numpy as jnp
from jax import lax
from jax.experimental import pallas as pl
from jax.experimental.pallas import tpu as pltpu

F32 = jnp.float32
BF16 = jnp.bfloat16

EPS = 1e-6
NEG_INF = -1e30
LOG2E = math.log2(math.e)
HEAD_DIM = 128
GRID_W = 64
NA_KH = 8
NA_KW = 16
N_HEADS_MEM = 4
DILATED_PAIRS = ((128, 1), (512, 4), (2048, 16))

VMEM_LIMIT = 52 * 1024 * 1024

_NT = (((1,), (1,)), ((), ()))


def _params(*sem):
    return pltpu.CompilerParams(dimension_semantics=sem, vmem_limit_bytes=VMEM_LIMIT)


def _rms(x, g):
    return x * lax.rsqrt(jnp.mean(x * x, axis=-1, keepdims=True) + EPS) * g


def _alibi_slopes(n):
    return np.exp2(-8.0 * np.arange(1, n + 1) / n).astype(np.float32)


def _layer_spec(layer, rows, cols, index_map):
    return pl.BlockSpec((None, rows, cols), lambda *g: (layer,) + tuple(index_map(*g)))


def _ffn_body(x_ref, g_ref, wg_ref, wu_ref, wd_ref, fg_ref, o_ref, h_ref, *, final_norm):
    j = pl.program_id(1)

    @pl.when(j == 0)
    def _():
        x = x_ref[...]
        h_ref[...] = _rms(x, g_ref[...]).astype(BF16)
        o_ref[...] = x

    h = h_ref[...]
    a = jnp.dot(h, wg_ref[...].astype(BF16), preferred_element_type=F32)
    b = jnp.dot(h, wu_ref[...].astype(BF16), preferred_element_type=F32)
    act = (0.5 * a * jax.nn.sigmoid(a)) * b
    o_ref[...] += jnp.dot(act.astype(BF16), wd_ref[...].astype(BF16), preferred_element_type=F32)

    if final_norm:
        @pl.when(j == pl.num_programs(1) - 1)
        def _():
            o_ref[...] = _rms(o_ref[...], fg_ref[...])


FFN_TF = 256
PROJ_TN = 1024


def _ffn(x, g, wg, wu, wd, layer, final_g=None, *, tm=1024):
    s, d = x.shape
    tf = FFN_TF
    f = wd.shape[1]
    final_norm = final_g is not None
    fg = final_g if final_norm else g
    return pl.pallas_call(
        functools.partial(_ffn_body, final_norm=final_norm),
        grid=(s // tm, f // tf),
        in_specs=[
            pl.BlockSpec((tm, d), lambda i, j: (i, 0)),
            pl.BlockSpec((1, d), lambda i, j: (0, 0)),
            _layer_spec(layer, d, tf, lambda i, j: (0, j)),
            _layer_spec(layer, d, tf, lambda i, j: (0, j)),
            _layer_spec(layer, tf, d, lambda i, j: (j, 0)),
            pl.BlockSpec((1, d), lambda i, j: (0, 0)),
        ],
        out_specs=pl.BlockSpec((tm, d), lambda i, j: (i, 0)),
        out_shape=jax.ShapeDtypeStruct((s, d), F32),
        scratch_shapes=[pltpu.VMEM((tm, d), BF16)],
        compiler_params=_params("parallel", "arbitrary"),
        name="ffn",
    )(x, g.reshape(1, d), wg, wu, wd, fg.reshape(1, d))


def _norm_proj_body(x_ref, g_ref, w_ref, cs_ref, o_ref, h_ref):
    @pl.when(pl.program_id(1) == 0)
    def _():
        h_ref[...] = _rms(x_ref[...], g_ref[...]).astype(BF16)

    acc = jnp.dot(h_ref[...], w_ref[...].astype(BF16), preferred_element_type=F32)
    o_ref[...] = (acc * cs_ref[...]).astype(o_ref.dtype)


def _norm_proj(x, g, w, layer, col_scale, *, tm=1024):
    s, d = x.shape
    tn = PROJ_TN
    n = w.shape[2]
    tm = min(tm, s)
    return pl.pallas_call(
        _norm_proj_body,
        grid=(s // tm, n // tn),
        in_specs=[
            pl.BlockSpec((tm, d), lambda i, j: (i, 0)),
            pl.BlockSpec((1, d), lambda i, j: (0, 0)),
            _layer_spec(layer, d, tn, lambda i, j: (0, j)),
            pl.BlockSpec((1, tn), lambda i, j: (0, j)),
        ],
        out_specs=pl.BlockSpec((tm, tn), lambda i, j: (i, j)),
        out_shape=jax.ShapeDtypeStruct((s, n), BF16),
        scratch_shapes=[pltpu.VMEM((tm, d), BF16)],
        compiler_params=_params("parallel", "arbitrary"),
        name="norm_proj",
    )(x, g.reshape(1, d), w, jnp.asarray(col_scale, F32).reshape(1, n))


def _col_scale(n, scaled):
    cs = np.ones((n,), np.float32)
    for lo, hi, val in scaled:
        cs[lo:hi] = val
    return cs


def _block_start(kb, t):
    return kb * t if isinstance(kb, int) else pl.multiple_of(kb * t, t)


ONES_ROWS = 16
UNROLL = 4
NA_UNROLL = 8
DIL_QBLOCKS = 4
DIFF_QBLOCKS = 4


def _values_transposed(src_ref, dst_ref, seq, chunk):
    def body(c, carry):
        r0 = pl.multiple_of(c * chunk, chunk)
        dst_ref[0:HEAD_DIM, pl.ds(r0, chunk)] = src_ref[pl.ds(r0, chunk), :].astype(F32).T.astype(BF16)
        return carry
    lax.fori_loop(0, seq // chunk, body, 0)
    dst_ref[HEAD_DIM:, :] = jnp.ones((ONES_ROWS, seq), BF16)


def _diff_body(slopes_ref, lam_ref, q_ref, k_ref, v_ref, sg_ref, o_ref,
               vt_ref, rdiag_ref, a_ref, u0_ref, u1_ref, mx0_ref, mx1_ref, acc_ref,
               *, t, seq, out_scale):
    h = pl.program_id(0)
    g = pl.program_id(1)
    nk = seq // t
    dk = HEAD_DIM // 2
    reps = t // HEAD_DIM
    slope = slopes_ref[h]
    lam = lam_ref[0]
    u_refs = (u0_ref, u1_ref)
    mx_refs = (mx0_ref, mx1_ref)

    @pl.when(g == 0)
    def _():
        _values_transposed(v_ref, vt_ref, seq, t)
        row = lax.broadcasted_iota(jnp.int32, (t, t), 0)
        col = lax.broadcasted_iota(jnp.int32, (t, t), 1)
        rdiag_ref[...] = -slope * jnp.abs(row - col).astype(F32)
        a = slope * lax.broadcasted_iota(jnp.int32, (t, HEAD_DIM), 0).astype(F32)
        a_ref[0] = a
        a_ref[1] = -a

    acc_ref[...] = jnp.zeros_like(acc_ref)
    q_off = slope * lax.broadcasted_iota(jnp.int32, (1, t), 1).astype(F32)
    lane = lax.broadcasted_iota(jnp.int32, (t, HEAD_DIM), 1)

    def q_halves(sub):
        q = q_ref[sub * t:(sub + 1) * t, :]
        zero = jnp.zeros_like(q)
        return jnp.where(lane < dk, q, zero), jnp.where(lane >= dk, q, zero)

    q_half = [q_halves(sub) for sub in range(DIFF_QBLOCKS)]

    def key_block(i, n):
        if isinstance(n, int) and n == 0:
            return i
        m = n - 1
        return jnp.where(n == 0, i, m + jnp.where(m >= i, 1, 0))

    def store_scores(u, slot, c):
        u_refs[slot][c] = u
        mx_refs[slot][c] = jnp.max(u, axis=0, keepdims=True)

    def diag_scores(sub, slot):
        i = DIFF_QBLOCKS * g + sub
        k = k_ref[pl.ds(_block_start(i, t), t), :]
        for c in range(2):
            s = lax.dot_general(k, q_half[sub][c], _NT, preferred_element_type=F32)
            store_scores(s + rdiag_ref[...], slot, c)

    def scores(sub, n, slot):
        i = DIFF_QBLOCKS * g + sub
        kb = key_block(i, n)
        k = k_ref[pl.ds(_block_start(kb, t), t), :]
        a = jnp.tile(a_ref[jnp.where(kb < i, 0, 1)], (1, reps))
        for c in range(2):
            s = lax.dot_general(k, q_half[sub][c], _NT, preferred_element_type=F32)
            store_scores(s + a, slot, c)

    def update(sub, n, slot, carry):
        i = DIFF_QBLOCKS * g + sub
        kb = key_block(i, n)
        vt = vt_ref[:, pl.ds(_block_start(kb, t), t)]
        off = slope * (jnp.abs(kb - i) * t).astype(F32)
        sign = jnp.where(kb < i, -1.0, jnp.where(kb > i, 1.0, 0.0))
        e = sign * q_off - off
        new = []
        for c in range(2):
            m_old = carry[c]
            m_new = jnp.maximum(m_old, mx_refs[slot][c] + e)
            alpha = jnp.exp2(m_old - m_new)
            p = jnp.exp2(u_refs[slot][c] - (m_new - e))
            acc_ref[sub, c] = (alpha * acc_ref[sub, c]
                               + jnp.dot(vt, p.astype(BF16), preferred_element_type=F32))
            new.append(m_new)
        return tuple(new)

    def finish(sub):
        o0 = acc_ref[sub, 0, 0:HEAD_DIM] / acc_ref[sub, 0, HEAD_DIM:HEAD_DIM + 1]
        o1 = acc_ref[sub, 1, 0:HEAD_DIM] / acc_ref[sub, 1, HEAD_DIM:HEAD_DIM + 1]
        ot = o0 - lam * o1
        o = _rms(ot.T, sg_ref[...]) * out_scale
        o_ref[sub * t:(sub + 1) * t, :] = o.astype(o_ref.dtype)

    m_init = jnp.full((1, t), NEG_INF, F32)
    diag_scores(0, 0)
    for sub in range(DIFF_QBLOCKS):
        def group(j, carry, sub=sub):
            for s in range(UNROLL):
                n = UNROLL * j + s
                scores(sub, n + 1, (s + 1) % 2)
                carry = update(sub, n, s % 2, carry)
            return carry

        carry = lax.fori_loop(0, nk // UNROLL - 1, group, (m_init, m_init))
        for n in range(nk - UNROLL, nk):
            if n + 1 < nk:
                scores(sub, n + 1, (n + 1) % 2)
            elif sub + 1 < DIFF_QBLOCKS:
                diag_scores(sub + 1, (n + 1) % 2)
            carry = update(sub, n, n % 2, carry)
        finish(sub)


def _diff_attention(qkv, lam, subln, lambda_init, *, n_heads, q_col, k_col, v_col, t=512):
    seq = qkv.shape[0]
    assert UNROLL % 2 == 0 and (seq // t) % UNROLL == 0 and (seq // t) % DIFF_QBLOCKS == 0
    slopes = jnp.asarray(_alibi_slopes(n_heads) * np.float32(LOG2E))
    smem = pl.BlockSpec(memory_space=pltpu.SMEM)
    tile = pltpu.VMEM((2, t, t), F32)
    colmax = pltpu.VMEM((2, 1, t), F32)
    return pl.pallas_call(
        functools.partial(_diff_body, t=t, seq=seq, out_scale=1.0 - lambda_init),
        grid=(n_heads, seq // (DIFF_QBLOCKS * t)),
        in_specs=[
            smem,
            smem,
            pl.BlockSpec((DIFF_QBLOCKS * t, HEAD_DIM), lambda h, i: (i, q_col + h)),
            pl.BlockSpec((seq, HEAD_DIM), lambda h, i: (0, k_col + h)),
            pl.BlockSpec((seq, HEAD_DIM), lambda h, i: (0, v_col + h)),
            pl.BlockSpec((1, HEAD_DIM), lambda h, i: (0, 0)),
        ],
        out_specs=pl.BlockSpec((DIFF_QBLOCKS * t, HEAD_DIM), lambda h, i: (i, h)),
        out_shape=jax.ShapeDtypeStruct((seq, n_heads * HEAD_DIM), BF16),
        scratch_shapes=[pltpu.VMEM((HEAD_DIM + ONES_ROWS, seq), BF16), pltpu.VMEM((t, t), F32),
                        pltpu.VMEM((2, t, HEAD_DIM), F32),
                        tile, tile, colmax, colmax,
                        pltpu.VMEM((DIFF_QBLOCKS, 2, HEAD_DIM + ONES_ROWS, t), F32)],
        compiler_params=_params("parallel", "arbitrary"),
        name="diff_attn",
    )(slopes, lam.reshape(1).astype(F32), qkv, qkv, qkv, subln.reshape(1, HEAD_DIM))


NA_ROWS_Q = 4
NA_ROWS_K = 12


def _na_bias_table(rpb, rows):
    kh, kw, w = NA_KH, NA_KW, GRID_W
    n_heads = rpb.shape[0]
    nblk = rows // NA_ROWS_Q
    rpb = rpb.astype(F32) * LOG2E
    ext_idx = np.clip(np.arange(2 * w - 1) - (w - 1) + kw - 1, 0, 2 * kw - 2)
    lo = int(np.argmax(ext_idx > 0))
    hi = int(np.argmax(ext_idx == 2 * kw - 2))
    ext = jnp.concatenate([jnp.repeat(rpb[..., :1], lo - 1, axis=-1), rpb,
                           jnp.repeat(rpb[..., -1:], 2 * w - 1 - hi - 1, axis=-1)], axis=-1)
    tab = jnp.stack([ext[..., w - 1 - qc:2 * w - 1 - qc] for qc in range(w)], axis=-2)
    c = np.arange(w)
    c0 = np.clip(c - kw // 2, 0, w - kw)
    col_ok = (c[None, :] >= c0[:, None]) & (c[None, :] < c0[:, None] + kw)
    tab = jnp.where(col_ok, tab, NEG_INF)
    cases = []
    for blk in (0, 1, nblk - 1):
        ws = int(np.clip(NA_ROWS_Q * blk - kh // 2, 0, rows - NA_ROWS_K))
        q_rows = []
        for a in range(NA_ROWS_Q):
            r = NA_ROWS_Q * blk + a
            r0 = int(np.clip(r - kh // 2, 0, rows - kh))
            lo = r0 - ws
            assert 0 <= lo and lo + kh <= NA_ROWS_K
            first = r0 - r + kh - 1
            band = tab[:, first:first + kh].transpose(0, 2, 1, 3).reshape(n_heads, w, kh * w)
            q_rows.append(jnp.pad(band, ((0, 0), (0, 0), (lo * w, (NA_ROWS_K - lo - kh) * w)),
                                  constant_values=NEG_INF))
        cases.append(jnp.concatenate(q_rows, axis=-2))
    return jnp.stack(cases, axis=1).swapaxes(-1, -2)


def _na_body(q_ref, k_ref, v_ref, b_ref, o_ref, vt_ref, u0_ref, u1_ref, *, rows):
    seq = rows * GRID_W
    nblk = rows // NA_ROWS_Q
    tq = NA_ROWS_Q * GRID_W
    nkeys = NA_ROWS_K * GRID_W
    u_refs = (u0_ref, u1_ref)
    _values_transposed(v_ref, vt_ref, seq, 512)

    def window(rb):
        ws = jnp.clip(NA_ROWS_Q * rb - NA_KH // 2, 0, rows - NA_ROWS_K)
        return pl.multiple_of(ws * GRID_W, NA_ROWS_Q * GRID_W)

    def scores(rb, slot):
        case = jnp.where(rb == 0, 0, jnp.where(rb == nblk - 1, 2, 1))
        k = k_ref[pl.ds(window(rb), nkeys), :]
        q = q_ref[pl.ds(_block_start(rb, tq), tq), :]
        u_refs[slot][...] = lax.dot_general(k, q, _NT, preferred_element_type=F32) + b_ref[case]

    def finish(rb, slot):
        u = u_refs[slot][...]
        p = jnp.exp2(u - jnp.max(u, axis=0, keepdims=True))
        vt = vt_ref[:, pl.ds(window(rb), nkeys)]
        acc = jnp.dot(vt, p.astype(BF16), preferred_element_type=F32)
        o = (acc[0:HEAD_DIM] / acc[HEAD_DIM:HEAD_DIM + 1]).T
        o_ref[pl.ds(_block_start(rb, tq), tq), :] = o.astype(o_ref.dtype)

    def group(j, carry):
        for s in range(NA_UNROLL):
            rb = NA_UNROLL * j + s
            scores(rb + 1, (s + 1) % 2)
            finish(rb, s % 2)
        return carry

    scores(0, 0)
    lax.fori_loop(0, nblk // NA_UNROLL - 1, group, 0)
    for rb in range(nblk - NA_UNROLL, nblk):
        if rb + 1 < nblk:
            scores(rb + 1, (rb + 1) % 2)
        finish(rb, rb % 2)


def _neighborhood_attention(qkv, rpb, *, n_heads, q_col, k_col, v_col):
    seq = qkv.shape[0]
    rows = seq // GRID_W
    assert NA_UNROLL % 2 == 0 and (rows // NA_ROWS_Q) % NA_UNROLL == 0 and NA_ROWS_Q == NA_KH // 2
    tq = NA_ROWS_Q * GRID_W
    nkeys = NA_ROWS_K * GRID_W
    bias = _na_bias_table(rpb, rows)
    head_cols = lambda col: pl.BlockSpec((seq, HEAD_DIM), lambda h: (0, col + h))
    return pl.pallas_call(
        functools.partial(_na_body, rows=rows),
        grid=(n_heads,),
        in_specs=[
            head_cols(q_col),
            head_cols(k_col),
            head_cols(v_col),
            pl.BlockSpec((None, 3, nkeys, tq), lambda h: (h, 0, 0, 0)),
        ],
        out_specs=head_cols(0),
        out_shape=jax.ShapeDtypeStruct((seq, n_heads * HEAD_DIM), BF16),
        scratch_shapes=[pltpu.VMEM((HEAD_DIM + ONES_ROWS, seq), BF16),
                        pltpu.VMEM((nkeys, tq), F32), pltpu.VMEM((nkeys, tq), F32)],
        compiler_params=_params("arbitrary"),
        name="na_attn",
    )(qkv, qkv, qkv, bias)


def _dilated_band_tables(t, nb):
    rel = np.arange(t)[:, None] - np.arange(t)[None, :]
    dist, logc = [], []
    for dlt in range(-nb, nb + 1):
        ad = np.abs(rel + dlt * t)
        cnt = sum(((ad <= window // 2) & (ad % dil == 0)).astype(np.int64) for window, dil in DILATED_PAIRS)
        dist.append(np.where(cnt > 0, ad, 0))
        logc.append(np.where(cnt > 0, np.log2(np.maximum(cnt, 1)), NEG_INF))
    return np.stack(dist).astype(np.float32), np.stack(logc).astype(np.float32)


def _dil_body(slopes_ref, q_ref, k_ref, v_ref, dist_ref, logc_ref, o_ref, vt_ref, b_ref, u0_ref, u1_ref,
              acc_ref, *, t, seq, nb):
    h = pl.program_id(0)
    g = pl.program_id(1)
    nk = seq // t
    nsteps = 2 * nb + 1
    slope = slopes_ref[h]
    u_refs = (u0_ref, u1_ref)

    @pl.when(g == 0)
    def _():
        _values_transposed(v_ref, vt_ref, seq, t)
        for s in range(nsteps):
            b_ref[s] = logc_ref[s] - slope * dist_ref[s]
        b_ref[nsteps] = jnp.full((t, t), NEG_INF, F32)

    acc_ref[...] = jnp.zeros_like(acc_ref)

    def key_block(sub, step):
        return DIL_QBLOCKS * g + sub - nb + step

    def scores(sub, step, slot):
        kb = key_block(sub, step)
        inside = (kb >= 0) & (kb < nk)
        k = k_ref[pl.ds(_block_start(jnp.clip(kb, 0, nk - 1), t), t), :]
        q = q_ref[sub * t:(sub + 1) * t, :]
        u = lax.dot_general(k, q, _NT, preferred_element_type=F32) + b_ref[jnp.where(inside, step, nsteps)]
        u_refs[slot][...] = u
        return jnp.max(u, axis=0, keepdims=True)

    def update(sub, step, slot, mx, m_old):
        kb = jnp.clip(key_block(sub, step), 0, nk - 1)
        vt = vt_ref[:, pl.ds(_block_start(kb, t), t)]
        m_new = jnp.maximum(m_old, mx)
        alpha = jnp.exp2(m_old - m_new)
        p = jnp.exp2(u_refs[slot][...] - m_new)
        acc_ref[sub] = alpha * acc_ref[sub] + jnp.dot(vt, p.astype(BF16), preferred_element_type=F32)
        return m_new

    order = [(sub, step) for sub in range(DIL_QBLOCKS) for step in range(nsteps)]
    m = [jnp.full((1, t), NEG_INF, F32)] * DIL_QBLOCKS
    mx = scores(*order[0], 0)
    for n, (sub, step) in enumerate(order):
        mx_next = scores(*order[n + 1], (n + 1) % 2) if n + 1 < len(order) else None
        m[sub] = update(sub, step, n % 2, mx, m[sub])
        mx = mx_next
        if step == nsteps - 1:
            o = (acc_ref[sub, 0:HEAD_DIM] / acc_ref[sub, HEAD_DIM:HEAD_DIM + 1]).T
            o_ref[sub * t:(sub + 1) * t, :] = o.astype(o_ref.dtype)


def _dilated_attention(qkv, *, n_heads, t=512):
    seq = qkv.shape[0]
    for window, dil in DILATED_PAIRS:
        assert dil & (dil - 1) == 0 and window % (2 * dil) == 0
    reach = max(window // 2 for window, _ in DILATED_PAIRS)
    nb = (reach - 1) // t + 1
    slopes = jnp.asarray(_alibi_slopes(n_heads) * np.float32(LOG2E))
    dist, logc = _dilated_band_tables(t, nb)
    table = pl.BlockSpec((2 * nb + 1, t, t), lambda h, i: (0, 0, 0), pipeline_mode=pl.Buffered(1))
    return pl.pallas_call(
        functools.partial(_dil_body, t=t, seq=seq, nb=nb),
        grid=(n_heads, seq // (DIL_QBLOCKS * t)),
        in_specs=[
            pl.BlockSpec(memory_space=pltpu.SMEM),
            pl.BlockSpec((DIL_QBLOCKS * t, HEAD_DIM), lambda h, i: (i, h)),
            pl.BlockSpec((seq, HEAD_DIM), lambda h, i: (0, n_heads + h)),
            pl.BlockSpec((seq, HEAD_DIM), lambda h, i: (0, 2 * n_heads + h)),
            table,
            table,
        ],
        out_specs=pl.BlockSpec((DIL_QBLOCKS * t, HEAD_DIM), lambda h, i: (i, h)),
        out_shape=jax.ShapeDtypeStruct((seq, n_heads * HEAD_DIM), BF16),
        scratch_shapes=[pltpu.VMEM((HEAD_DIM + ONES_ROWS, seq), BF16), pltpu.VMEM((2 * nb + 2, t, t), F32),
                        pltpu.VMEM((t, t), F32), pltpu.VMEM((t, t), F32),
                        pltpu.VMEM((DIL_QBLOCKS, HEAD_DIM + ONES_ROWS, t), F32)],
        compiler_params=_params("parallel", "arbitrary"),
        name="dilated_attn",
    )(slopes, qkv, qkv, qkv, jnp.asarray(dist), jnp.asarray(logc))


def _post_mix_body(a_ref, b_ref, wa_ref, wb_ref, x_ref, g_ref, wq_ref, kv_ref, wo_ref, o_ref, *, scale):
    y = x_ref[...] + jnp.dot(a_ref[...], wa_ref[...], preferred_element_type=F32)
    y = y + jnp.dot(b_ref[...], wb_ref[...], preferred_element_type=F32)
    h = _rms(y, g_ref[...]).astype(BF16)
    q = jnp.dot(h, wq_ref[...], preferred_element_type=F32).astype(BF16)
    width = N_HEADS_MEM * HEAD_DIM
    outs = []
    for hd in range(N_HEADS_MEM):
        lo, hi = hd * HEAD_DIM, (hd + 1) * HEAD_DIM
        s = lax.dot_general(q[:, lo:hi], kv_ref[:, lo:hi], _NT, preferred_element_type=F32) * scale
        m = jnp.max(s, axis=-1, keepdims=True)
        p = jnp.exp(s - m)
        l = jnp.sum(p, axis=-1, keepdims=True)
        o = jnp.dot(p.astype(BF16), kv_ref[:, width + lo:width + hi], preferred_element_type=F32) / l
        outs.append(o.astype(BF16))
    o = jnp.concatenate(outs, axis=-1)
    o_ref[...] = y + jnp.dot(o, wo_ref[...], preferred_element_type=F32)


def _post_mix(a, a_col, b, b_col, w_out, x, g, wq, kv, wo, layer, *, tm=512):
    s, d = x.shape
    kh = w_out.shape[1] // 2
    n_mem, kvw = kv.shape
    width = wq.shape[2]
    once = pl.Buffered(1)

    def resident(rows, cols, r):
        return pl.BlockSpec((None, rows, cols), lambda i: (layer, r, 0), pipeline_mode=once)

    return pl.pallas_call(
        functools.partial(_post_mix_body, scale=HEAD_DIM ** -0.5),
        grid=(s // tm,),
        in_specs=[
            pl.BlockSpec((tm, kh), lambda i: (i, a_col)),
            pl.BlockSpec((tm, kh), lambda i: (i, b_col)),
            resident(kh, d, 0),
            resident(kh, d, 1),
            pl.BlockSpec((tm, d), lambda i: (i, 0)),
            pl.BlockSpec((1, d), lambda i: (0, 0)),
            resident(d, width, 0),
            pl.BlockSpec((n_mem, kvw), lambda i: (0, 0), pipeline_mode=once),
            resident(width, d, 0),
        ],
        out_specs=pl.BlockSpec((tm, d), lambda i: (i, 0)),
        out_shape=jax.ShapeDtypeStruct((s, d), F32),
        compiler_params=_params("parallel"),
        name="post_mix",
    )(a, b, w_out, w_out, x, g.reshape(1, d), wq, kv, wo)


def kernel(x, mem, ffn1_norm, ffn1_w_gate, ffn1_w_up, ffn1_w_down, mix_norm, mix_w_in, mix_w_out, diff_lq1, diff_lk1, diff_lq2, diff_lk2, diff_subln, na_rpb, mem_q_norm, mem_kv_norm, mem_wq, mem_wkv, mem_wo, ffn2_norm, ffn2_w_gate, ffn2_w_up, ffn2_w_down, final_norm):
    batch, seq, d_model = x.shape
    assert batch == 1
    depth = ffn1_norm.shape[0]
    n_heads = d_model // HEAD_DIM
    in_proj = mix_w_in.shape[2]
    xs = x.reshape(seq, d_model)
    mem2 = mem.reshape(mem.shape[1], d_model)

    bf = lambda t: t.astype(BF16)
    w1g, w1u, w1d = ffn1_w_gate, ffn1_w_up, ffn1_w_down
    w2g, w2u, w2d = ffn2_w_gate, ffn2_w_up, ffn2_w_down
    w_in, w_out = mix_w_in, bf(mix_w_out)
    wq, wkv, wo = bf(mem_wq), mem_wkv, bf(mem_wo)

    hh = n_heads // 2
    half_w = hh * HEAD_DIM
    cs_even = _col_scale(in_proj, [(0, half_w, (HEAD_DIM // 2) ** -0.5 * LOG2E),
                                   (3 * half_w, 4 * half_w, HEAD_DIM ** -0.5 * LOG2E)])
    cs_odd = _col_scale(in_proj, [(0, d_model, HEAD_DIM ** -0.5 * LOG2E)])
    cs_kv = _col_scale(mem_wkv.shape[2], [])

    for i in range(depth):
        xs = _ffn(xs, ffn1_norm[i], w1g, w1u, w1d, i)
        if i % 2 == 0:
            e = i // 2
            qkv = _norm_proj(xs, mix_norm[i], w_in, i, cs_even)
            lambda_init = 0.8 - 0.6 * math.exp(-0.3 * i)
            lam = (jnp.exp(jnp.sum(diff_lq1[e].astype(F32) * diff_lk1[e].astype(F32)))
                   - jnp.exp(jnp.sum(diff_lq2[e].astype(F32) * diff_lk2[e].astype(F32)))
                   + lambda_init)
            oa = _diff_attention(qkv, lam, diff_subln[e], lambda_init, n_heads=hh,
                                 q_col=0, k_col=hh, v_col=2 * hh)
            ob = _neighborhood_attention(qkv, na_rpb[e], n_heads=hh,
                                         q_col=3 * hh, k_col=4 * hh, v_col=5 * hh)
            mix = (oa, 0, ob, 0)
        else:
            qkv = _norm_proj(xs, mix_norm[i], w_in, i, cs_odd)
            o = _dilated_attention(qkv, n_heads=n_heads)
            mix = (o, 0, o, 1)
        kv = _norm_proj(mem2, mem_kv_norm[i], wkv, i, cs_kv)
        xs = _post_mix(*mix, w_out, xs, mem_q_norm[i], wq, kv, wo, i)
        last = i == depth - 1
        xs = _ffn(xs, ffn2_norm[i], w2g, w2u, w2d, i, final_g=final_norm if last else None)
    return xs.reshape(batch, seq, d_model)
```

```python
import functools
import math

import numpy as np
import jax
import jax.numpy as jnp
from jax import lax
from jax.experimental import pallas as pl
from jax.experimental.pallas import tpu as pltpu

F32 = jnp.float32
BF16 = jnp.bfloat16

EPS = 1e-6
NEG_INF = -1e30
LOG2E = math.log2(math.e)
HEAD_DIM = 128
GRID_W = 64
NA_KH = 8
NA_KW = 16
N_HEADS_MEM = 4
DILATED_PAIRS = ((128, 1), (512, 4), (2048, 16))

VMEM_LIMIT = 52 * 1024 * 1024

_NT = (((1,), (1,)), ((), ()))


def _params(*sem):
    return pltpu.CompilerParams(dimension_semantics=sem, vmem_limit_bytes=VMEM_LIMIT)


def _rms(x, g):
    return x * lax.rsqrt(jnp.mean(x * x, axis=-1, keepdims=True) + EPS) * g


def _alibi_slopes(n):
    return np.exp2(-8.0 * np.arange(1, n + 1) / n).astype(np.float32)


def _layer_spec(layer, rows, cols, index_map):
    return pl.BlockSpec((None, rows, cols), lambda *g: (layer,) + tuple(index_map(*g)))


def _ffn_body(x_ref, g_ref, wg_ref, wu_ref, wd_ref, fg_ref, o_ref, h_ref, *, final_norm):
    j = pl.program_id(1)

    @pl.when(j == 0)
    def _():
        x = x_ref[...]
        h_ref[...] = _rms(x, g_ref[...]).astype(BF16)
        o_ref[...] = x

    h = h_ref[...]
    a = jnp.dot(h, wg_ref[...].astype(BF16), preferred_element_type=F32)
    b = jnp.dot(h, wu_ref[...].astype(BF16), preferred_element_type=F32)
    act = (0.5 * a * jax.nn.sigmoid(a)) * b
    o_ref[...] += jnp.dot(act.astype(BF16), wd_ref[...].astype(BF16), preferred_element_type=F32)

    if final_norm:
        @pl.when(j == pl.num_programs(1) - 1)
        def _():
            o_ref[...] = _rms(o_ref[...], fg_ref[...])


FFN_TF = 256
PROJ_TN = 1024


def _ffn(x, g, wg, wu, wd, layer, final_g=None, *, tm=1024):
    s, d = x.shape
    tf = FFN_TF
    f = wd.shape[1]
    final_norm = final_g is not None
    fg = final_g if final_norm else g
    return pl.pallas_call(
        functools.partial(_ffn_body, final_norm=final_norm),
        grid=(s // tm, f // tf),
        in_specs=[
            pl.BlockSpec((tm, d), lambda i, j: (i, 0)),
            pl.BlockSpec((1, d), lambda i, j: (0, 0)),
            _layer_spec(layer, d, tf, lambda i, j: (0, j)),
            _layer_spec(layer, d, tf, lambda i, j: (0, j)),
            _layer_spec(layer, tf, d, lambda i, j: (j, 0)),
            pl.BlockSpec((1, d), lambda i, j: (0, 0)),
        ],
        out_specs=pl.BlockSpec((tm, d), lambda i, j: (i, 0)),
        out_shape=jax.ShapeDtypeStruct((s, d), F32),
        scratch_shapes=[pltpu.VMEM((tm, d), BF16)],
        compiler_params=_params("parallel", "arbitrary"),
        name="ffn",
    )(x, g.reshape(1, d), wg, wu, wd, fg.reshape(1, d))


def _norm_proj_body(x_ref, g_ref, w_ref, cs_ref, o_ref, h_ref):
    @pl.when(pl.program_id(1) == 0)
    def _():
        h_ref[...] = _rms(x_ref[...], g_ref[...]).astype(BF16)

    acc = jnp.dot(h_ref[...], w_ref[...].astype(BF16), preferred_element_type=F32)
    o_ref[...] = (acc * cs_ref[...]).astype(o_ref.dtype)


def _norm_proj(x, g, w, layer, col_scale, *, tm=1024):
    s, d = x.shape
    tn = PROJ_TN
    n = w.shape[2]
    tm = min(tm, s)
    return pl.pallas_call(
        _norm_proj_body,
        grid=(s // tm, n // tn),
        in_specs=[
            pl.BlockSpec((tm, d), lambda i, j: (i, 0)),
            pl.BlockSpec((1, d), lambda i, j: (0, 0)),
            _layer_spec(layer, d, tn, lambda i, j: (0, j)),
            pl.BlockSpec((1, tn), lambda i, j: (0, j)),
        ],
        out_specs=pl.BlockSpec((tm, tn), lambda i, j: (i, j)),
        out_shape=jax.ShapeDtypeStruct((s, n), BF16),
        scratch_shapes=[pltpu.VMEM((tm, d), BF16)],
        compiler_params=_params("parallel", "arbitrary"),
        name="norm_proj",
    )(x, g.reshape(1, d), w, jnp.asarray(col_scale, F32).reshape(1, n))


def _col_scale(n, scaled):
    cs = np.ones((n,), np.float32)
    for lo, hi, val in scaled:
        cs[lo:hi] = val
    return cs


def _block_start(kb, t):
    return kb * t if isinstance(kb, int) else pl.multiple_of(kb * t, t)


ONES_ROWS = 16
UNROLL = 4
NA_UNROLL = 8
DIL_QBLOCKS = 4
DIFF_QBLOCKS = 4


def _values_transposed(src_ref, dst_ref, seq, chunk):
    def body(c, carry):
        r0 = pl.multiple_of(c * chunk, chunk)
        dst_ref[0:HEAD_DIM, pl.ds(r0, chunk)] = src_ref[pl.ds(r0, chunk), :].astype(F32).T.astype(BF16)
        return carry
    lax.fori_loop(0, seq // chunk, body, 0)
    dst_ref[HEAD_DIM:, :] = jnp.ones((ONES_ROWS, seq), BF16)


def _diff_body(slopes_ref, lam_ref, q_ref, k_ref, v_ref, sg_ref, o_ref,
               vt_ref, rdiag_ref, a_ref, u0_ref, u1_ref, mx0_ref, mx1_ref, acc_ref,
               *, t, seq, out_scale):
    h = pl.program_id(0)
    g = pl.program_id(1)
    nk = seq // t
    dk = HEAD_DIM // 2
    reps = t // HEAD_DIM
    slope = slopes_ref[h]
    lam = lam_ref[0]
    u_refs = (u0_ref, u1_ref)
    mx_refs = (mx0_ref, mx1_ref)

    @pl.when(g == 0)
    def _():
        _values_transposed(v_ref, vt_ref, seq, t)
        row = lax.broadcasted_iota(jnp.int32, (t, t), 0)
        col = lax.broadcasted_iota(jnp.int32, (t, t), 1)
        rdiag_ref[...] = -slope * jnp.abs(row - col).astype(F32)
        a = slope * lax.broadcasted_iota(jnp.int32, (t, HEAD_DIM), 0).astype(F32)
        a_ref[0] = a
        a_ref[1] = -a

    acc_ref[...] = jnp.zeros_like(acc_ref)
    q_off = slope * lax.broadcasted_iota(jnp.int32, (1, t), 1).astype(F32)
    lane = lax.broadcasted_iota(jnp.int32, (t, HEAD_DIM), 1)

    def q_halves(sub):
        q = q_ref[sub * t:(sub + 1) * t, :]
        zero = jnp.zeros_like(q)
        return jnp.where(lane < dk, q, zero), jnp.where(lane >= dk, q, zero)

    q_half = [q_halves(sub) for sub in range(DIFF_QBLOCKS)]

    def key_block(i, n):
        if isinstance(n, int) and n == 0:
            return i
        m = n - 1
        return jnp.where(n == 0, i, m + jnp.where(m >= i, 1, 0))

    def store_scores(u, slot, c):
        u_refs[slot][c] = u
        mx_refs[slot][c] = jnp.max(u, axis=0, keepdims=True)

    def diag_scores(sub, slot):
        i = DIFF_QBLOCKS * g + sub
        k = k_ref[pl.ds(_block_start(i, t), t), :]
        for c in range(2):
            s = lax.dot_general(k, q_half[sub][c], _NT, preferred_element_type=F32)
            store_scores(s + rdiag_ref[...], slot, c)

    def scores(sub, n, slot):
        i = DIFF_QBLOCKS * g + sub
        kb = key_block(i, n)
        k = k_ref[pl.ds(_block_start(kb, t), t), :]
        a = jnp.tile(a_ref[jnp.where(kb < i, 0, 1)], (1, reps))
        for c in range(2):
            s = lax.dot_general(k, q_half[sub][c], _NT, preferred_element_type=F32)
            store_scores(s + a, slot, c)

    def update(sub, n, slot, carry):
        i = DIFF_QBLOCKS * g + sub
        kb = key_block(i, n)
        vt = vt_ref[:, pl.ds(_block_start(kb, t), t)]
        off = slope * (jnp.abs(kb - i) * t).astype(F32)
        sign = jnp.where(kb < i, -1.0, jnp.where(kb > i, 1.0, 0.0))
        e = sign * q_off - off
        new = []
        for c in range(2):
            m_old = carry[c]
            m_new = jnp.maximum(m_old, mx_refs[slot][c] + e)
            alpha = jnp.exp2(m_old - m_new)
            p = jnp.exp2(u_refs[slot][c] - (m_new - e))
            acc_ref[sub, c] = (alpha * acc_ref[sub, c]
                               + jnp.dot(vt, p.astype(BF16), preferred_element_type=F32))
            new.append(m_new)
        return tuple(new)

    def finish(sub):
        o0 = acc_ref[sub, 0, 0:HEAD_DIM] / acc_ref[sub, 0, HEAD_DIM:HEAD_DIM + 1]
        o1 = acc_ref[sub, 1, 0:HEAD_DIM] / acc_ref[sub, 1, HEAD_DIM:HEAD_DIM + 1]
        ot = o0 - lam * o1
        o = _rms(ot.T, sg_ref[...]) * out_scale
        o_ref[sub * t:(sub + 1) * t, :] = o.astype(o_ref.dtype)

    m_init = jnp.full((1, t), NEG_INF, F32)
    diag_scores(0, 0)
    for sub in range(DIFF_QBLOCKS):
        def group(j, carry, sub=sub):
            for s in range(UNROLL):
                n = UNROLL * j + s
                scores(sub, n + 1, (s + 1) % 2)
                carry = update(sub, n, s % 2, carry)
            return carry

        carry = lax.fori_loop(0, nk // UNROLL - 1, group, (m_init, m_init))
        for n in range(nk - UNROLL, nk):
            if n + 1 < nk:
                scores(sub, n + 1, (n + 1) % 2)
            elif sub + 1 < DIFF_QBLOCKS:
                diag_scores(sub + 1, (n + 1) % 2)
            carry = update(sub, n, n % 2, carry)
        finish(sub)


def _diff_attention(qkv, lam, subln, lambda_init, *, n_heads, q_col, k_col, v_col, t=512):
    seq = qkv.shape[0]
    assert UNROLL % 2 == 0 and (seq // t) % UNROLL == 0 and (seq // t) % DIFF_QBLOCKS == 0
    slopes = jnp.asarray(_alibi_slopes(n_heads) * np.float32(LOG2E))
    smem = pl.BlockSpec(memory_space=pltpu.SMEM)
    tile = pltpu.VMEM((2, t, t), F32)
    colmax = pltpu.VMEM((2, 1, t), F32)
    return pl.pallas_call(
        functools.partial(_diff_body, t=t, seq=seq, out_scale=1.0 - lambda_init),
        grid=(n_heads, seq // (DIFF_QBLOCKS * t)),
        in_specs=[
            smem,
            smem,
            pl.BlockSpec((DIFF_QBLOCKS * t, HEAD_DIM), lambda h, i: (i, q_col + h)),
            pl.BlockSpec((seq, HEAD_DIM), lambda h, i: (0, k_col + h)),
            pl.BlockSpec((seq, HEAD_DIM), lambda h, i: (0, v_col + h)),
            pl.BlockSpec((1, HEAD_DIM), lambda h, i: (0, 0)),
        ],
        out_specs=pl.BlockSpec((DIFF_QBLOCKS * t, HEAD_DIM), lambda h, i: (i, h)),
        out_shape=jax.ShapeDtypeStruct((seq, n_heads * HEAD_DIM), BF16),
        scratch_shapes=[pltpu.VMEM((HEAD_DIM + ONES_ROWS, seq), BF16), pltpu.VMEM((t, t), F32),
                        pltpu.VMEM((2, t, HEAD_DIM), F32),
                        tile, tile, colmax, colmax,
                        pltpu.VMEM((DIFF_QBLOCKS, 2, HEAD_DIM + ONES_ROWS, t), F32)],
        compiler_params=_params("parallel", "arbitrary"),
        name="diff_attn",
    )(slopes, lam.reshape(1).astype(F32), qkv, qkv, qkv, subln.reshape(1, HEAD_DIM))


NA_ROWS_Q = 4
NA_ROWS_K = 12


def _na_bias_table(rpb, rows):
    kh, kw, w = NA_KH, NA_KW, GRID_W
    n_heads = rpb.shape[0]
    nblk = rows // NA_ROWS_Q
    rpb = rpb.astype(F32) * LOG2E
    ext_idx = np.clip(np.arange(2 * w - 1) - (w - 1) + kw - 1, 0, 2 * kw - 2)
    lo = int(np.argmax(ext_idx > 0))
    hi = int(np.argmax(ext_idx == 2 * kw - 2))
    ext = jnp.concatenate([jnp.repeat(rpb[..., :1], lo - 1, axis=-1), rpb,
                           jnp.repeat(rpb[..., -1:], 2 * w - 1 - hi - 1, axis=-1)], axis=-1)
    tab = jnp.stack([ext[..., w - 1 - qc:2 * w - 1 - qc] for qc in range(w)], axis=-2)
    c = np.arange(w)
    c0 = np.clip(c - kw // 2, 0, w - kw)
    col_ok = (c[None, :] >= c0[:, None]) & (c[None, :] < c0[:, None] + kw)
    tab = jnp.where(col_ok, tab, NEG_INF)
    cases = []
    for blk in (0, 1, nblk - 1):
        ws = int(np.clip(NA_ROWS_Q * blk - kh // 2, 0, rows - NA_ROWS_K))
        q_rows = []
        for a in range(NA_ROWS_Q):
            r = NA_ROWS_Q * blk + a
            r0 = int(np.clip(r - kh // 2, 0, rows - kh))
            lo = r0 - ws
            assert 0 <= lo and lo + kh <= NA_ROWS_K
            first = r0 - r + kh - 1
            band = tab[:, first:first + kh].transpose(0, 2, 1, 3).reshape(n_heads, w, kh * w)
            q_rows.append(jnp.pad(band, ((0, 0), (0, 0), (lo * w, (NA_ROWS_K - lo - kh) * w)),
                                  constant_values=NEG_INF))
        cases.append(jnp.concatenate(q_rows, axis=-2))
    return jnp.stack(cases, axis=1).swapaxes(-1, -2)


def _na_body(q_ref, k_ref, v_ref, b_ref, o_ref, vt_ref, u0_ref, u1_ref, mx0_ref, mx1_ref, *, rows):
    seq = rows * GRID_W
    nblk = rows // NA_ROWS_Q
    tq = NA_ROWS_Q * GRID_W
    nkeys = NA_ROWS_K * GRID_W
    u_refs = (u0_ref, u1_ref)
    mx_refs = (mx0_ref, mx1_ref)
    _values_transposed(v_ref, vt_ref, seq, 512)

    def window(rb):
        ws = jnp.clip(NA_ROWS_Q * rb - NA_KH // 2, 0, rows - NA_ROWS_K)
        return pl.multiple_of(ws * GRID_W, NA_ROWS_Q * GRID_W)

    def scores(rb, slot):
        case = jnp.where(rb == 0, 0, jnp.where(rb == nblk - 1, 2, 1))
        k = k_ref[pl.ds(window(rb), nkeys), :]
        q = q_ref[pl.ds(_block_start(rb, tq), tq), :]
        u = lax.dot_general(k, q, _NT, preferred_element_type=F32) + b_ref[case]
        u_refs[slot][...] = u
        mx_refs[slot][...] = jnp.max(u, axis=0, keepdims=True)

    def finish(rb, slot):
        p = jnp.exp2(u_refs[slot][...] - mx_refs[slot][...])
        vt = vt_ref[:, pl.ds(window(rb), nkeys)]
        acc = jnp.dot(vt, p.astype(BF16), preferred_element_type=F32)
        o = (acc[0:HEAD_DIM] / acc[HEAD_DIM:HEAD_DIM + 1]).T
        o_ref[pl.ds(_block_start(rb, tq), tq), :] = o.astype(o_ref.dtype)

    def group(j, carry):
        for s in range(NA_UNROLL):
            rb = NA_UNROLL * j + s
            scores(rb + 1, (s + 1) % 2)
            finish(rb, s % 2)
        return carry

    scores(0, 0)
    lax.fori_loop(0, nblk // NA_UNROLL - 1, group, 0)
    for rb in range(nblk - NA_UNROLL, nblk):
        if rb + 1 < nblk:
            scores(rb + 1, (rb + 1) % 2)
        finish(rb, rb % 2)


def _neighborhood_attention(qkv, rpb, *, n_heads, q_col, k_col, v_col):
    seq = qkv.shape[0]
    rows = seq // GRID_W
    assert NA_UNROLL % 2 == 0 and (rows // NA_ROWS_Q) % NA_UNROLL == 0 and NA_ROWS_Q == NA_KH // 2
    tq = NA_ROWS_Q * GRID_W
    nkeys = NA_ROWS_K * GRID_W
    bias = _na_bias_table(rpb, rows)
    head_cols = lambda col: pl.BlockSpec((seq, HEAD_DIM), lambda h: (0, col + h))
    return pl.pallas_call(
        functools.partial(_na_body, rows=rows),
        grid=(n_heads,),
        in_specs=[
            head_cols(q_col),
            head_cols(k_col),
            head_cols(v_col),
            pl.BlockSpec((None, 3, nkeys, tq), lambda h: (h, 0, 0, 0)),
        ],
        out_specs=head_cols(0),
        out_shape=jax.ShapeDtypeStruct((seq, n_heads * HEAD_DIM), BF16),
        scratch_shapes=[pltpu.VMEM((HEAD_DIM + ONES_ROWS, seq), BF16),
                        pltpu.VMEM((nkeys, tq), F32), pltpu.VMEM((nkeys, tq), F32),
                        pltpu.VMEM((1, tq), F32), pltpu.VMEM((1, tq), F32)],
        compiler_params=_params("arbitrary"),
        name="na_attn",
    )(qkv, qkv, qkv, bias)


def _dilated_band_tables(t, nb):
    rel = np.arange(t)[:, None] - np.arange(t)[None, :]
    dist, logc = [], []
    for dlt in range(-nb, nb + 1):
        ad = np.abs(rel + dlt * t)
        cnt = sum(((ad <= window // 2) & (ad % dil == 0)).astype(np.int64) for window, dil in DILATED_PAIRS)
        dist.append(np.where(cnt > 0, ad, 0))
        logc.append(np.where(cnt > 0, np.log2(np.maximum(cnt, 1)), NEG_INF))
    return np.stack(dist).astype(np.float32), np.stack(logc).astype(np.float32)


def _dil_body(slopes_ref, q_ref, k_ref, v_ref, dist_ref, logc_ref, o_ref, vt_ref, b_ref, u0_ref, u1_ref,
              acc_ref, *, t, seq, nb):
    h = pl.program_id(0)
    g = pl.program_id(1)
    nk = seq // t
    nsteps = 2 * nb + 1
    slope = slopes_ref[h]
    u_refs = (u0_ref, u1_ref)

    @pl.when(g == 0)
    def _():
        _values_transposed(v_ref, vt_ref, seq, t)
        for s in range(nsteps):
            b_ref[s] = logc_ref[s] - slope * dist_ref[s]
        b_ref[nsteps] = jnp.full((t, t), NEG_INF, F32)

    acc_ref[...] = jnp.zeros_like(acc_ref)

    def key_block(sub, step):
        return DIL_QBLOCKS * g + sub - nb + step

    def scores(sub, step, slot):
        kb = key_block(sub, step)
        inside = (kb >= 0) & (kb < nk)
        k = k_ref[pl.ds(_block_start(jnp.clip(kb, 0, nk - 1), t), t), :]
        q = q_ref[sub * t:(sub + 1) * t, :]
        u = lax.dot_general(k, q, _NT, preferred_element_type=F32) + b_ref[jnp.where(inside, step, nsteps)]
        u_refs[slot][...] = u
        return jnp.max(u, axis=0, keepdims=True)

    def update(sub, step, slot, mx, m_old):
        kb = jnp.clip(key_block(sub, step), 0, nk - 1)
        vt = vt_ref[:, pl.ds(_block_start(kb, t), t)]
        m_new = jnp.maximum(m_old, mx)
        alpha = jnp.exp2(m_old - m_new)
        p = jnp.exp2(u_refs[slot][...] - m_new)
        acc_ref[sub] = alpha * acc_ref[sub] + jnp.dot(vt, p.astype(BF16), preferred_element_type=F32)
        return m_new

    order = [(sub, step) for sub in range(DIL_QBLOCKS) for step in range(nsteps)]
    m = [jnp.full((1, t), NEG_INF, F32)] * DIL_QBLOCKS
    mx = scores(*order[0], 0)
    for n, (sub, step) in enumerate(order):
        mx_next = scores(*order[n + 1], (n + 1) % 2) if n + 1 < len(order) else None
        m[sub] = update(sub, step, n % 2, mx, m[sub])
        mx = mx_next
        if step == nsteps - 1:
            o = (acc_ref[sub, 0:HEAD_DIM] / acc_ref[sub, HEAD_DIM:HEAD_DIM + 1]).T
            o_ref[sub * t:(sub + 1) * t, :] = o.astype(o_ref.dtype)


def _dilated_attention(qkv, *, n_heads, t=512):
    seq = qkv.shape[0]
    for window, dil in DILATED_PAIRS:
        assert dil & (dil - 1) == 0 and window % (2 * dil) == 0
    reach = max(window // 2 for window, _ in DILATED_PAIRS)
    nb = (reach - 1) // t + 1
    slopes = jnp.asarray(_alibi_slopes(n_heads) * np.float32(LOG2E))
    dist, logc = _dilated_band_tables(t, nb)
    table = pl.BlockSpec((2 * nb + 1, t, t), lambda h, i: (0, 0, 0), pipeline_mode=pl.Buffered(1))
    return pl.pallas_call(
        functools.partial(_dil_body, t=t, seq=seq, nb=nb),
        grid=(n_heads, seq // (DIL_QBLOCKS * t)),
        in_specs=[
            pl.BlockSpec(memory_space=pltpu.SMEM),
            pl.BlockSpec((DIL_QBLOCKS * t, HEAD_DIM), lambda h, i: (i, h)),
            pl.BlockSpec((seq, HEAD_DIM), lambda h, i: (0, n_heads + h)),
            pl.BlockSpec((seq, HEAD_DIM), lambda h, i: (0, 2 * n_heads + h)),
            table,
            table,
        ],
        out_specs=pl.BlockSpec((DIL_QBLOCKS * t, HEAD_DIM), lambda h, i: (i, h)),
        out_shape=jax.ShapeDtypeStruct((seq, n_heads * HEAD_DIM), BF16),
        scratch_shapes=[pltpu.VMEM((HEAD_DIM + ONES_ROWS, seq), BF16), pltpu.VMEM((2 * nb + 2, t, t), F32),
                        pltpu.VMEM((t, t), F32), pltpu.VMEM((t, t), F32),
                        pltpu.VMEM((DIL_QBLOCKS, HEAD_DIM + ONES_ROWS, t), F32)],
        compiler_params=_params("parallel", "arbitrary"),
        name="dilated_attn",
    )(slopes, qkv, qkv, qkv, jnp.asarray(dist), jnp.asarray(logc))


def _post_mix_body(a_ref, b_ref, wa_ref, wb_ref, x_ref, g_ref, wq_ref, kv_ref, wo_ref, o_ref, *, scale):
    y = x_ref[...] + jnp.dot(a_ref[...], wa_ref[...], preferred_element_type=F32)
    y = y + jnp.dot(b_ref[...], wb_ref[...], preferred_element_type=F32)
    h = _rms(y, g_ref[...]).astype(BF16)
    q = jnp.dot(h, wq_ref[...], preferred_element_type=F32).astype(BF16)
    width = N_HEADS_MEM * HEAD_DIM
    outs = []
    for hd in range(N_HEADS_MEM):
        lo, hi = hd * HEAD_DIM, (hd + 1) * HEAD_DIM
        s = lax.dot_general(q[:, lo:hi], kv_ref[:, lo:hi], _NT, preferred_element_type=F32) * scale
        m = jnp.max(s, axis=-1, keepdims=True)
        p = jnp.exp(s - m)
        l = jnp.sum(p, axis=-1, keepdims=True)
        o = jnp.dot(p.astype(BF16), kv_ref[:, width + lo:width + hi], preferred_element_type=F32) / l
        outs.append(o.astype(BF16))
    o = jnp.concatenate(outs, axis=-1)
    o_ref[...] = y + jnp.dot(o, wo_ref[...], preferred_element_type=F32)


def _post_mix(a, a_col, b, b_col, w_out, x, g, wq, kv, wo, layer, *, tm=512):
    s, d = x.shape
    kh = w_out.shape[1] // 2
    n_mem, kvw = kv.shape
    width = wq.shape[2]
    once = pl.Buffered(1)

    def resident(rows, cols, r):
        return pl.BlockSpec((None, rows, cols), lambda i: (layer, r, 0), pipeline_mode=once)

    return pl.pallas_call(
        functools.partial(_post_mix_body, scale=HEAD_DIM ** -0.5),
        grid=(s // tm,),
        in_specs=[
            pl.BlockSpec((tm, kh), lambda i: (i, a_col)),
            pl.BlockSpec((tm, kh), lambda i: (i, b_col)),
            resident(kh, d, 0),
            resident(kh, d, 1),
            pl.BlockSpec((tm, d), lambda i: (i, 0)),
            pl.BlockSpec((1, d), lambda i: (0, 0)),
            resident(d, width, 0),
            pl.BlockSpec((n_mem, kvw), lambda i: (0, 0), pipeline_mode=once),
            resident(width, d, 0),
        ],
        out_specs=pl.BlockSpec((tm, d), lambda i: (i, 0)),
        out_shape=jax.ShapeDtypeStruct((s, d), F32),
        compiler_params=_params("parallel"),
        name="post_mix",
    )(a, b, w_out, w_out, x, g.reshape(1, d), wq, kv, wo)


def kernel(x, mem, ffn1_norm, ffn1_w_gate, ffn1_w_up, ffn1_w_down, mix_norm, mix_w_in, mix_w_out, diff_lq1, diff_lk1, diff_lq2, diff_lk2, diff_subln, na_rpb, mem_q_norm, mem_kv_norm, mem_wq, mem_wkv, mem_wo, ffn2_norm, ffn2_w_gate, ffn2_w_up, ffn2_w_down, final_norm):
    batch, seq, d_model = x.shape
    assert batch == 1
    depth = ffn1_norm.shape[0]
    n_heads = d_model // HEAD_DIM
    in_proj = mix_w_in.shape[2]
    xs = x.reshape(seq, d_model)
    mem2 = mem.reshape(mem.shape[1], d_model)

    bf = lambda t: t.astype(BF16)
    w1g, w1u, w1d = ffn1_w_gate, ffn1_w_up, ffn1_w_down
    w2g, w2u, w2d = ffn2_w_gate, ffn2_w_up, ffn2_w_down
    w_in, w_out = mix_w_in, bf(mix_w_out)
    wq, wkv, wo = bf(mem_wq), mem_wkv, bf(mem_wo)

    hh = n_heads // 2
    half_w = hh * HEAD_DIM
    cs_even = _col_scale(in_proj, [(0, half_w, (HEAD_DIM // 2) ** -0.5 * LOG2E),
                                   (3 * half_w, 4 * half_w, HEAD_DIM ** -0.5 * LOG2E)])
    cs_odd = _col_scale(in_proj, [(0, d_model, HEAD_DIM ** -0.5 * LOG2E)])
    cs_kv = _col_scale(mem_wkv.shape[2], [])

    for i in range(depth):
        xs = _ffn(xs, ffn1_norm[i], w1g, w1u, w1d, i)
        if i % 2 == 0:
            e = i // 2
            qkv = _norm_proj(xs, mix_norm[i], w_in, i, cs_even)
            lambda_init = 0.8 - 0.6 * math.exp(-0.3 * i)
            lam = (jnp.exp(jnp.sum(diff_lq1[e].astype(F32) * diff_lk1[e].astype(F32)))
                   - jnp.exp(jnp.sum(diff_lq2[e].astype(F32) * diff_lk2[e].astype(F32)))
                   + lambda_init)
            oa = _diff_attention(qkv, lam, diff_subln[e], lambda_init, n_heads=hh,
                                 q_col=0, k_col=hh, v_col=2 * hh)
            ob = _neighborhood_attention(qkv, na_rpb[e], n_heads=hh,
                                         q_col=3 * hh, k_col=4 * hh, v_col=5 * hh)
            mix = (oa, 0, ob, 0)
        else:
            qkv = _norm_proj(xs, mix_norm[i], w_in, i, cs_odd)
            o = _dilated_attention(qkv, n_heads=n_heads)
            mix = (o, 0, o, 1)
        kv = _norm_proj(mem2, mem_kv_norm[i], wkv, i, cs_kv)
        xs = _post_mix(*mix, w_out, xs, mem_q_norm[i], wq, kv, wo, i)
        last = i == depth - 1
        xs = _ffn(xs, ffn2_norm[i], w2g, w2u, w2d, i, final_g=final_norm if last else None)
    return xs.reshape(batch, seq, d_model)
```
